```python
import math
import jax
import jax.numpy as jnp
from jax import lax
import numpy as np

D_MODEL = 1024
BATCH = 4
SEQ = 8192
DEPTH = 2

GRID_W = 64
CTX_LEN = 256
EPS = 1e-6
D_HYENA = D_MODEL // 2
HYENA_ORDER = 2
SHORT_CONV = 3
FILTER_EMB = 33
FILTER_WIDTH = 64
FILTER_OUT_STD = 0.005
DECAY_FAST = 0.3
DECAY_SLOW = 1.5
DECAY_TARGET = 1e-2
DECAY_SHIFT = 0.05
DA_HEADS = 4
DA_HEAD_DIM = 64
D_DIFF = DA_HEADS * 2 * DA_HEAD_DIM
ROPE_BASE = 10000.0
Q_BLOCK = 128
N_HQ = 3 * D_HYENA + D_DIFF
D_SGU = D_MODEL
SGU_GROUPS = 8
CHUNK = 128
D_FF = 2816
FFN_CONV = 3

kernel_name = "hybrid_hyena_diffattn_sgu_dit"


def rmsnorm(x, g):
    xf = x.astype(jnp.float32)
    y = xf * lax.rsqrt(jnp.mean(xf * xf, axis=-1, keepdims=True) + EPS)
    return (y * g.astype(jnp.float32)).astype(x.dtype)


def layernorm(x, g, b):
    xf = x.astype(jnp.float32)
    mu = jnp.mean(xf, axis=-1, keepdims=True)
    xc = xf - mu
    y = xc * lax.rsqrt(jnp.mean(xc * xc, axis=-1, keepdims=True) + EPS)
    return (y * g.astype(jnp.float32) + b.astype(jnp.float32)).astype(x.dtype)


def modulate(h, shift, scale):
    return h * (1 + scale) + shift


def dwconv3(u, w, b):
    up = jnp.pad(u, ((0, 0), (1, 1), (0, 0)))
    return up[:, :-2] * w[0] + up[:, 1:-1] * w[1] + up[:, 2:] * w[2] + b


def hyena_filters(L, w1, b1, freq, w2, b2, w3):
    f32 = jnp.float32
    bands = (FILTER_EMB - 1) // 2
    pos = jnp.arange(L, dtype=f32)
    t = jnp.linspace(0.0, 1.0, L, dtype=f32)[:, None]
    ang = (2.0 * math.pi / L) * pos[:, None] * jnp.linspace(1e-4, bands - 1, bands, dtype=f32)[None, :]
    z = jnp.concatenate([t, jnp.cos(ang), -jnp.sin(ang)], axis=-1)
    h = jnp.sin(freq[0].astype(f32) * (z @ w1.astype(f32) + b1.astype(f32)))
    h = jnp.sin(freq[1].astype(f32) * (h @ w2.astype(f32) + b2.astype(f32)))
    h = h @ w3.astype(f32)
    deltas = jnp.abs(jnp.linspace(math.log(DECAY_TARGET) / DECAY_SLOW, math.log(DECAY_TARGET) / DECAY_FAST, D_HYENA, dtype=f32))
    window = jnp.exp(-t * deltas[None, :]) + DECAY_SHIFT
    h = h.reshape(L, HYENA_ORDER, 2, D_HYENA) * window[:, None, None, :]
    fwd, bwd = h[:, :, 0], h[:, :, 1]
    return jnp.concatenate([fwd, jnp.zeros_like(fwd[:1]), bwd[:0:-1]], axis=0)


def fftconv_bidir(u, h_full, d_skip):
    L = u.shape[1]
    uf32 = u.astype(jnp.float32)
    uf = jnp.fft.rfft(uf32, n=2 * L, axis=1)
    hf = jnp.fft.rfft(h_full, n=2 * L, axis=0)
    y = jnp.fft.irfft(uf * hf[None], n=2 * L, axis=1)[:, :L]
    return (y + uf32 * d_skip.astype(jnp.float32)).astype(u.dtype)


def hyena_mix(proj, conv_w, conv_b, w1, b1, freq, w2, b2, w3, d_skip):
    L = proj.shape[1]
    proj = dwconv3(proj, conv_w, conv_b)
    v, x1, x2 = jnp.split(proj, 3, axis=-1)
    filt = hyena_filters(L, w1, b1, freq, w2, b2, w3)
    z = v
    for o, gate in enumerate((x1, x2)):
        z = gate * fftconv_bidir(z, filt[:, o], d_skip[o])
    return z


def rope_1d(x, pos):
    m = x.shape[-1] // 2
    inv = ROPE_BASE ** (-jnp.arange(m, dtype=jnp.float32) / m)
    ang = pos.astype(jnp.float32)[:, None] * inv[None, :]
    cos, sin = jnp.cos(ang).astype(x.dtype), jnp.sin(ang).astype(x.dtype)
    x1, x2 = x[..., :m], x[..., m:]
    return jnp.concatenate([x1 * cos - x2 * sin, x1 * sin + x2 * cos], axis=-1)


def rope_2d(x, row, col):
    half = x.shape[-1] // 2
    return jnp.concatenate([rope_1d(x[..., :half], row), rope_1d(x[..., half:], col)], axis=-1)


def to_qk_heads(t):
    B, L, _ = t.shape
    return t.reshape(B, L, DA_HEADS, 2, DA_HEAD_DIM).transpose(0, 2, 3, 1, 4)


def to_v_heads(t):
    B, L, _ = t.shape
    return t.reshape(B, L, DA_HEADS, 2 * DA_HEAD_DIM).transpose(0, 2, 1, 3)


def diff_attention(q, k, v, lam):
    B, H, _, Lq, dh = q.shape
    nb = Lq // Q_BLOCK
    qb = q.reshape(B, H, 2, nb, Q_BLOCK, dh).transpose(3, 0, 1, 2, 4, 5)
    scale = dh ** -0.5

    def block(qi):
        s = jnp.einsum('bhmqd,bhmkd->bhmqk', qi, k).astype(jnp.float32) * scale
        p = jax.nn.softmax(s, axis=-1)
        a = (p[:, :, 0] - lam * p[:, :, 1]).astype(v.dtype)
        return jnp.einsum('bhqk,bhkd->bhqd', a, v)

    o = lax.map(block, qb)
    return o.transpose(1, 2, 0, 3, 4).reshape(B, H, Lq, 2 * dh)


def ab_mixer(h, w_in, w_out, hy, lam, lam_init, subln_g, kc, vc, pos):
    B, L, _ = h.shape
    if pos is None:
        proj = h @ w_in[:, :N_HQ]
        k, v = kc, vc
    else:
        proj = h @ w_in
        kv = proj[..., N_HQ:]
        k = jnp.concatenate([kc, rope_2d(to_qk_heads(kv[..., :D_DIFF]), *pos)], axis=3)
        v = jnp.concatenate([vc, to_v_heads(kv[..., D_DIFF:])], axis=2)
    y_a = hyena_mix(proj[..., :3 * D_HYENA], *hy)
    q = to_qk_heads(proj[..., 3 * D_HYENA:N_HQ])
    if pos is not None:
        q = rope_2d(q, *pos)
    o = diff_attention(q, k, v, lam)
    o = rmsnorm(o, subln_g) * (1.0 - lam_init)
    y_b = o.transpose(0, 2, 1, 3).reshape(B, L, D_DIFF)
    return jnp.concatenate([y_a, y_b], axis=-1) @ w_out


def sgu_mixer(h, w_in, b_in, ln_g, ln_b, w_s, b_s, w_out):
    B, L, _ = h.shape
    u, v = jnp.split(jax.nn.gelu(h @ w_in + b_in, approximate=False), 2, axis=-1)
    v = layernorm(v, ln_g, ln_b)
    n = L // CHUNK
    vb = v.reshape(B, n, CHUNK, SGU_GROUPS, D_SGU // SGU_GROUPS)
    s = jnp.einsum('gts,bnsgc->bntgc', w_s, vb) + b_s.T[:, :, None]
    return (u * s.reshape(B, L, D_SGU)) @ w_out


def conv_ffn(h, w_up, conv_w, conv_b, w_down):
    a = dwconv3(h @ w_up, conv_w, conv_b)
    g, u = jnp.split(a, 2, axis=-1)
    return (jax.nn.silu(g) * u) @ w_down


def setup_inputs(seed: int = 0) -> dict:
    key = jax.random.key(seed)
    ks = iter(jax.random.split(key, 40))
    D = D_MODEL
    n_even = (DEPTH + 1) // 2
    n_odd = DEPTH // 2

    def nrm(shape, scale):
        return jax.random.normal(next(ks), shape, jnp.float32) * scale

    def gain(shape):
        return 1.0 + nrm(shape, 0.02)

    return {
        "x": nrm((BATCH, SEQ, D), 1.0),
        "c": nrm((BATCH, D), 1.0),
        "ctx": nrm((BATCH, CTX_LEN, D), 1.0),
        "c_ctx": nrm((D,), 1.0),
        "mod_w": nrm((DEPTH, D, 6 * D), D ** -0.5),
        "mod_b": nrm((DEPTH, 6 * D), 0.02),
        "norm_mix_g": gain((DEPTH, D)),
        "norm_ffn_g": gain((DEPTH, D)),
        "ffn_w_up": nrm((DEPTH, D, 2 * D_FF), D ** -0.5),
        "ffn_conv_w": nrm((DEPTH, FFN_CONV, 2 * D_FF), FFN_CONV ** -0.5),
        "ffn_conv_b": nrm((DEPTH, 2 * D_FF), 0.02),
        "ffn_w_down": nrm((DEPTH, D_FF, D), D_FF ** -0.5),
        "ab_w_in": nrm((n_even, D, 3 * D_HYENA + 3 * D_DIFF), D ** -0.5),
        "hy_conv_w": nrm((n_even, SHORT_CONV, 3 * D_HYENA), SHORT_CONV ** -0.5),
        "hy_conv_b": nrm((n_even, 3 * D_HYENA), 0.02),
        "hy_w1": nrm((n_even, FILTER_EMB, FILTER_WIDTH), FILTER_EMB ** -0.5),
        "hy_b1": nrm((n_even, FILTER_WIDTH), 0.1),
        "hy_freq": gain((n_even, 2, FILTER_WIDTH)),
        "hy_w2": nrm((n_even, FILTER_WIDTH, FILTER_WIDTH), FILTER_WIDTH ** -0.5),
        "hy_b2": nrm((n_even, FILTER_WIDTH), 0.1),
        "hy_w3": nrm((n_even, FILTER_WIDTH, HYENA_ORDER * 2 * D_HYENA), FILTER_OUT_STD),
        "hy_bias": nrm((n_even, HYENA_ORDER, D_HYENA), 1.0),
        "da_lambda": nrm((n_even, 4, DA_HEAD_DIM), 0.1),
        "da_subln_g": gain((n_even, 2 * DA_HEAD_DIM)),
        "ab_w_out": nrm((n_even, D_HYENA + D_DIFF, D), (D_HYENA + D_DIFF) ** -0.5),
        "sgu_w_in": nrm((n_odd, D, 2 * D_SGU), D ** -0.5),
        "sgu_b_in": nrm((n_odd, 2 * D_SGU), 0.02),
        "sgu_ln_g": gain((n_odd, D_SGU)),
        "sgu_ln_b": nrm((n_odd, D_SGU), 0.02),
        "sgu_w_s": nrm((n_odd, SGU_GROUPS, CHUNK, CHUNK), CHUNK ** -0.5),
        "sgu_b_s": gain((n_odd, SGU_GROUPS, CHUNK)),
        "sgu_w_out": nrm((n_odd, D_SGU, D), D_SGU ** -0.5),
        "final_norm_g": gain((D,)),
    }


def reference(x, c, ctx, c_ctx, mod_w, mod_b, norm_mix_g, norm_ffn_g, ffn_w_up, ffn_conv_w, ffn_conv_b,
              ffn_w_down, ab_w_in, hy_conv_w, hy_conv_b, hy_w1, hy_b1, hy_freq, hy_w2, hy_b2, hy_w3, hy_bias,
              da_lambda, da_subln_g, ab_w_out, sgu_w_in, sgu_b_in, sgu_ln_g, sgu_ln_b, sgu_w_s, sgu_b_s,
              sgu_w_out, final_norm_g):
    D = D_MODEL
    S = x.shape[1]
    rows = S // GRID_W
    row = jnp.repeat(jnp.arange(rows, dtype=jnp.int32), GRID_W)
    col = jnp.tile(jnp.arange(GRID_W, dtype=jnp.int32), rows)
    for i in range(DEPTH):
        ctx_live = any(j % 2 == 0 for j in range(i + 1, DEPTH))
        need_ctx = (i % 2 == 0) or ctx_live
        sh1, sc1, g1, sh2, sc2, g2 = (m[:, None, :] for m in jnp.split(jax.nn.silu(c) @ mod_w[i] + mod_b[i], 6, axis=-1))
        hx = modulate(rmsnorm(x, norm_mix_g[i]), sh1, sc1)
        if need_ctx:
            n_mod = 6 if ctx_live else 2
            cmod = jnp.split(jax.nn.silu(c_ctx) @ mod_w[i][:, :n_mod * D] + mod_b[i][:n_mod * D], n_mod)
            hc = modulate(rmsnorm(ctx, norm_mix_g[i]), cmod[0], cmod[1])
        if i % 2 == 0:
            e = i // 2
            lam_init = 0.8 - 0.6 * math.exp(-0.3 * i)
            lq1, lk1, lq2, lk2 = da_lambda[e].astype(jnp.float32)
            lam = jnp.exp(jnp.sum(lq1 * lk1)) - jnp.exp(jnp.sum(lq2 * lk2)) + lam_init
            hy = (hy_conv_w[e], hy_conv_b[e], hy_w1[e], hy_b1[e], hy_freq[e], hy_w2[e], hy_b2[e], hy_w3[e], hy_bias[e])
            kvc = hc @ ab_w_in[e][:, N_HQ:]
            kc, vc = to_qk_heads(kvc[..., :D_DIFF]), to_v_heads(kvc[..., D_DIFF:])
            mix_x = ab_mixer(hx, ab_w_in[e], ab_w_out[e], hy, lam, lam_init, da_subln_g[e], kc, vc, (row, col))
            if ctx_live:
                mix_c = ab_mixer(hc, ab_w_in[e], ab_w_out[e], hy, lam, lam_init, da_subln_g[e], kc, vc, None)
        else:
            o = i // 2
            sgu = (sgu_w_in[o], sgu_b_in[o], sgu_ln_g[o], sgu_ln_b[o], sgu_w_s[o], sgu_b_s[o], sgu_w_out[o])
            mix_x = sgu_mixer(hx, *sgu)
            if ctx_live:
                mix_c = sgu_mixer(hc, *sgu)
        ffn = (ffn_w_up[i], ffn_conv_w[i], ffn_conv_b[i], ffn_w_down[i])
        x = x + g1 * mix_x
        x = x + g2 * conv_ffn(modulate(rmsnorm(x, norm_ffn_g[i]), sh2, sc2), *ffn)
        if ctx_live:
            ctx = ctx + cmod[2] * mix_c
            ctx = ctx + cmod[5] * conv_ffn(modulate(rmsnorm(ctx, norm_ffn_g[i]), cmod[3], cmod[4]), *ffn)
    return rmsnorm(x, final_norm_g)
```

```python
import functools
import math

import numpy as np
import jax
import jax.numpy as jnp
from jax import lax
from jax.experimental import pallas as pl
from jax.experimental.pallas import tpu as pltpu

D = 1024
B = 4
S = 8192
T = B * S
GRID_W = 64
CTX = 256
EPS = 1e-6
DH = 512
DD = 512
HEADS = 4
FILTER_EMB = 33
FILTER_EMB_PAD = 64
FILTER_WIDTH = 64
D_FF = 2816
CHUNK = 128
GROUPS = 8

NFFT = 2 * S
R = 128
HALF = R // 2
WCOL = R * DH

F32 = jnp.float32
BF16 = jnp.bfloat16
HI = lax.Precision.HIGHEST

VMEM_LIMIT = 56 * 1024 * 1024


def _params(*sem):
    return pltpu.CompilerParams(dimension_semantics=sem, vmem_limit_bytes=VMEM_LIMIT)


def _rms_mod(x, g, sh, sc):
    y = x * lax.rsqrt(jnp.mean(x * x, axis=-1, keepdims=True) + EPS)
    return (y * g) * (1.0 + sc) + sh


def _silu(x):
    return x * (1.0 / (1.0 + jnp.exp(-x)))


def _dot(a, b, **kw):
    return jnp.dot(a, b, preferred_element_type=F32, **kw)


def _mod_kernel(c_ref, w_ref, b_ref, o_ref):
    o_ref[...] = _dot(_silu(c_ref[...]), w_ref[...], precision=HI) + b_ref[...]


def _mod_call(cc, mod_w, mod_b):
    depth = mod_w.shape[0]
    tn = 1024
    return pl.pallas_call(
        _mod_kernel,
        grid=(depth, 6 * D // tn),
        in_specs=[
            pl.BlockSpec((8, D), lambda l, j: (0, 0)),
            pl.BlockSpec((None, D, tn), lambda l, j: (l, 0, j)),
            pl.BlockSpec((None, 1, tn), lambda l, j: (l, 0, j)),
        ],
        out_specs=pl.BlockSpec((None, 8, tn), lambda l, j: (l, 0, j)),
        out_shape=jax.ShapeDtypeStruct((depth, 8, 6 * D), F32),
        compiler_params=_params("parallel", "parallel"),
        name="mod",
    )(cc, mod_w, mod_b.reshape(depth, 1, 6 * D))


TM_IN = 512
HALO = 8


def _dot_nt(a, b):
    return lax.dot_general(a, b, (((1,), (1,)), ((), ())), preferred_element_type=F32)


def _inproj_kernel(xp_ref, xm_ref, xn_ref, g_ref, sh_ref, sc_ref, why_ref, wqk_ref, wvt_ref, cos_ref, sin_ref,
                   cw_ref, cb_ref, hv_ref, hx1_ref, hx2_ref, q_ref, k_ref, vt_ref, buf_ref):
    i = pl.program_id(0)
    tiles = S // TM_IN
    first = (i % tiles) == 0
    last = (i % tiles) == tiles - 1
    g, sh, sc = g_ref[...], sh_ref[...], sc_ref[...]
    hn = _rms_mod(xm_ref[...], g, sh, sc).astype(BF16)
    hp = jnp.where(first, 0.0, _rms_mod(xp_ref[...], g, sh, sc)).astype(BF16)
    hx = jnp.where(last, 0.0, _rms_mod(xn_ref[...], g, sh, sc)).astype(BF16)
    why = why_ref[...]
    buf_ref[0:HALO, :] = _dot(hp, why)
    buf_ref[HALO:HALO + TM_IN, :] = _dot(hn, why)
    buf_ref[HALO + TM_IN:2 * HALO + TM_IN, :] = _dot(hx, why)
    cw = cw_ref[...]
    conv = (cw[0:1] * buf_ref[HALO - 1:HALO - 1 + TM_IN, :] + cw[1:2] * buf_ref[HALO:HALO + TM_IN, :]
            + cw[2:3] * buf_ref[HALO + 1:HALO + 1 + TM_IN, :] + cb_ref[...])
    hv_ref[...] = conv[:, 0:DH]
    hx1_ref[...] = conv[:, DH:2 * DH]
    hx2_ref[...] = conv[:, 2 * DH:3 * DH]

    qkv = _dot(hn, wqk_ref[...])
    vt_ref[...] = _dot_nt(wvt_ref[...], hn).astype(BF16)
    cos, sin = cos_ref[...], sin_ref[...]
    lane = lax.broadcasted_iota(jnp.int32, (TM_IN, 128), 1)
    even = ((lane // 16) % 2) == 0

    def rope(xg):
        partner = jnp.where(even, pltpu.roll(xg, 112, 1), pltpu.roll(xg, 16, 1))
        return xg * cos + partner * sin

    scale = 64 ** -0.5
    for h in range(HEADS):
        q_ref[:, h * 128:(h + 1) * 128] = (rope(qkv[:, h * 128:(h + 1) * 128]) * scale).astype(BF16)
        k_ref[:, h * 128:(h + 1) * 128] = rope(qkv[:, DD + h * 128:DD + (h + 1) * 128]).astype(BF16)


def _inproj_call(x2, g, sh, sc, why, wqk, wvt, cos_t, sin_t, cw, cb):
    tm = TM_IN
    tiles = S // tm
    nh = tm // HALO
    row = lambda i: (i, 0)
    bvec = pl.BlockSpec((None, 1, D), lambda i: (i // tiles, 0, 0))
    full = lambda shape: pl.BlockSpec(shape, lambda i: (0,) * len(shape))
    return pl.pallas_call(
        _inproj_kernel,
        grid=(T // tm,),
        in_specs=[
            pl.BlockSpec((HALO, D), lambda i: (jnp.maximum(i * nh - 1, 0), 0)),
            pl.BlockSpec((tm, D), row),
            pl.BlockSpec((HALO, D), lambda i: (jnp.minimum((i + 1) * nh, T // HALO - 1), 0)),
            full((1, D)), bvec, bvec,
            full((D, 3 * DH)), full((D, 2 * DD)), full((DD, D)),
            pl.BlockSpec((tm, 128), lambda i: (i % tiles, 0)),
            pl.BlockSpec((tm, 128), lambda i: (i % tiles, 0)),
            full((3, 3 * DH)), full((1, 3 * DH)),
        ],
        out_specs=[pl.BlockSpec((tm, DH), row)] * 3 + [pl.BlockSpec((tm, DD), row)] * 2
        + [pl.BlockSpec((None, DD, tm), lambda i: (i // tiles, 0, i % tiles))],
        out_shape=[jax.ShapeDtypeStruct((T, DH), F32)] * 3 + [jax.ShapeDtypeStruct((T, DD), BF16)] * 2
        + [jax.ShapeDtypeStruct((B, DD, S), BF16)],
        scratch_shapes=[pltpu.VMEM((tm + 2 * HALO, 3 * DH), F32)],
        compiler_params=_params("parallel"),
        name="inproj",
    )(x2, x2, x2, g, sh, sc, why, wqk, wvt, cos_t, sin_t, cw, cb)


def _ctxkv_kernel(x_ref, g_ref, sh_ref, sc_ref, wk_ref, wvt_ref, k_ref, vt_ref):
    hn = _rms_mod(x_ref[...], g_ref[...], sh_ref[...], sc_ref[...]).astype(BF16)
    k_ref[...] = _dot(hn, wk_ref[...]).astype(BF16)
    vt_ref[...] = _dot_nt(wvt_ref[...], hn).astype(BF16)


def _ctxkv_call(ctx2, g, sh, sc, wk, wvt):
    tm = CTX
    full = lambda shape: pl.BlockSpec(shape, lambda i: (0,) * len(shape))
    return pl.pallas_call(
        _ctxkv_kernel,
        grid=(B * CTX // tm,),
        in_specs=[pl.BlockSpec((tm, D), lambda i: (i, 0)), full((1, D)), full((1, D)), full((1, D)),
                  full((D, DD)), full((DD, D))],
        out_specs=[pl.BlockSpec((tm, DD), lambda i: (i, 0)), pl.BlockSpec((None, DD, tm), lambda i: (i, 0, 0))],
        out_shape=[jax.ShapeDtypeStruct((B * CTX, DD), BF16), jax.ShapeDtypeStruct((B, DD, CTX), BF16)],
        compiler_params=_params("parallel"),
        name="ctxkv",
    )(ctx2, g, sh, sc, wk, wvt)


TQ = 256
TK = 1024
LAM_INIT0 = 0.8 - 0.6 * math.exp(-0.3 * 0)


def _attn_kernel(lam_ref, g_ref, q_ref, k_ref, vt_ref, kc_ref, vct_ref, o_ref, s_ref, acc_ref):
    q = q_ref[...]
    lane = lax.broadcasted_iota(jnp.int32, (TQ, 128), 1)
    lo = lane < 64
    zero = jnp.zeros_like(q)
    qs = (jnp.where(lo, q, zero), jnp.where(lo, zero, q))

    def scores(kblk, off, n):
        part = []
        for mi in range(2):
            s = _dot_nt(kblk, qs[mi])
            s_ref[mi, pl.ds(off, n), :] = s
            part.append(jnp.max(s.reshape(n // 8, 8, TQ), axis=0))
        return part

    def body_a(j, carry):
        off = pl.multiple_of(j * TK, TK)
        part = scores(k_ref[pl.ds(off, TK), :], CTX + off, TK)
        return jnp.maximum(carry[0], part[0]), jnp.maximum(carry[1], part[1])

    m8 = lax.fori_loop(0, S // TK, body_a, tuple(scores(kc_ref[...], 0, CTX)))
    m = [jnp.max(m8[mi], axis=0, keepdims=True) for mi in range(2)]

    def probs(vtblk, off, n, first):
        part = []
        for mi in range(2):
            p = jnp.exp(s_ref[mi, pl.ds(off, n), :] - m[mi])
            part.append(jnp.sum(p.reshape(n // 8, 8, TQ), axis=0))
            pv = _dot(vtblk, p.astype(BF16))
            if first:
                acc_ref[mi] = pv
            else:
                acc_ref[mi] += pv
        return part

    def body_b(j, carry):
        off = pl.multiple_of(j * TK, TK)
        part = probs(vt_ref[:, pl.ds(off, TK)], CTX + off, TK, False)
        return carry[0] + part[0], carry[1] + part[1]

    l8 = lax.fori_loop(0, S // TK, body_b, tuple(probs(vct_ref[...], 0, CTX, True)))
    l = [jnp.sum(l8[mi], axis=0, keepdims=True) for mi in range(2)]

    lp = lam_ref[...]
    lam = (jnp.exp(jnp.sum(lp[0:1] * lp[1:2], axis=1, keepdims=True))
           - jnp.exp(jnp.sum(lp[2:3] * lp[3:4], axis=1, keepdims=True)) + LAM_INIT0)
    o = acc_ref[0] / l[0] - lam * (acc_ref[1] / l[1])
    o = o * lax.rsqrt(jnp.mean(o * o, axis=0, keepdims=True) + EPS)
    o = (o * g_ref[...]) * (1.0 - LAM_INIT0)
    o_ref[...] = o.T.astype(o_ref.dtype)


def _attn_call(lam_p, subln_g_col, q, k, vt, kc, vct):
    qspec = pl.BlockSpec((None, TQ, 128), lambda b, h, i: (b, i, h))
    return pl.pallas_call(
        _attn_kernel,
        grid=(B, HEADS, S // TQ),
        in_specs=[pl.BlockSpec((4, 64), lambda b, h, i: (0, 0)), pl.BlockSpec((128, 1), lambda b, h, i: (0, 0)),
                  qspec,
                  pl.BlockSpec((None, S, 128), lambda b, h, i: (b, 0, h)),
                  pl.BlockSpec((None, 128, S), lambda b, h, i: (b, h, 0)),
                  pl.BlockSpec((None, CTX, 128), lambda b, h, i: (b, 0, h)),
                  pl.BlockSpec((None, 128, CTX), lambda b, h, i: (b, h, 0))],
        out_specs=qspec,
        out_shape=jax.ShapeDtypeStruct((B, S, DD), BF16),
        scratch_shapes=[pltpu.VMEM((2, CTX + S, TQ), F32), pltpu.VMEM((2, 128, TQ), F32)],
        compiler_params=_params("parallel", "parallel", "arbitrary"),
        name="diffattn",
    )(lam_p, subln_g_col, q, k, vt, kc, vct)


def _cplx_block(m):
    return np.block([[m.real, -m.imag], [m.imag, m.real]])


@functools.lru_cache(maxsize=None)
def _dft_tables():
    k = np.arange(R)
    w_r = np.exp(-2j * np.pi * np.outer(k, k) / R)
    f_a = _cplx_block(w_r[:, :HALF])
    f_a_real = np.concatenate([w_r.real, w_r.imag], axis=0)
    g_a = _cplx_block(np.conj(w_r)[:HALF, :])
    tw = np.exp(-2j * np.pi * np.outer(k, k) / NFFT)
    m_f = w_r[None, :, :] * tw[:, None, :]
    m_fwd = np.stack([_cplx_block(m_f[i]) for i in range(R)])
    m_inv = np.stack([_cplx_block(np.conj(m_f[i]).T) for i in range(R)])
    return (f_a.astype(np.float32), f_a_real.astype(np.float32), g_a.astype(np.float32),
            m_fwd.astype(np.float32), m_inv.astype(np.float32))


def _filter_positions():
    idx = np.arange(NFFT)
    pos = np.where(idx < S, idx, NFFT - idx) % S
    bands = (FILTER_EMB - 1) // 2
    posf = jnp.arange(S, dtype=F32)
    t = jnp.linspace(0.0, 1.0, S, dtype=F32)[:, None]
    ang = (2.0 * math.pi / S) * posf[:, None] * jnp.linspace(1e-4, bands - 1, bands, dtype=F32)[None, :]
    z = jnp.concatenate([t, jnp.cos(ang), -jnp.sin(ang)], axis=-1)
    z = jnp.pad(z, ((0, 0), (0, FILTER_EMB_PAD - FILTER_EMB)))
    return z[pos]


def _decay_rates():
    return jnp.abs(jnp.linspace(math.log(1e-2) / 1.5, math.log(1e-2) / 0.3, DH, dtype=F32))[None, :]


def _rope_tables():
    rows = S // GRID_W
    row = jnp.repeat(jnp.arange(rows, dtype=jnp.int32), GRID_W).astype(F32)
    col = jnp.tile(jnp.arange(GRID_W, dtype=jnp.int32), rows).astype(F32)
    m = 16
    inv = 10000.0 ** (-jnp.arange(m, dtype=F32) / m)
    ang_r = row[:, None] * inv[None, :]
    ang_c = col[:, None] * inv[None, :]
    cos = jnp.concatenate([jnp.cos(ang_r)] * 2 + [jnp.cos(ang_c)] * 2, axis=1)
    sin = jnp.concatenate([-jnp.sin(ang_r), jnp.sin(ang_r), -jnp.sin(ang_c), jnp.sin(ang_c)], axis=1)
    return jnp.concatenate([cos, cos], axis=1), jnp.concatenate([sin, sin], axis=1)


TR_F = 512


def _filter_kernel(z_ref, w1_ref, b1_ref, fr_ref, w2_ref, b2_ref, w3_ref, dec_ref, o_ref):
    i = pl.program_id(0)
    z = z_ref[...]
    fr = fr_ref[...]
    h = jnp.sin(fr[0:1] * (_dot(z, w1_ref[...], precision=HI) + b1_ref[...]))
    h = jnp.sin(fr[1:2] * (_dot(h, w2_ref[...], precision=HI) + b2_ref[...]))
    h = _dot(h, w3_ref[...], precision=HI)
    window = jnp.exp(-z[:, 0:1] * dec_ref[...]) + 0.05
    ridx = i * TR_F + lax.broadcasted_iota(jnp.int32, (TR_F, 1), 0)
    o_ref[...] = jnp.where(ridx == S, 0.0, h * window)


def _filter_call(zfull, w1p, b1, freq, w2, b2, w3r, dec):
    full = lambda shape: pl.BlockSpec(shape, lambda i, o: (0,) * len(shape))
    half_tiles = S // TR_F
    return pl.pallas_call(
        _filter_kernel,
        grid=(NFFT // TR_F, 2),
        in_specs=[pl.BlockSpec((TR_F, FILTER_EMB_PAD), lambda i, o: (i, 0)),
                  full((FILTER_EMB_PAD, FILTER_WIDTH)), full((1, FILTER_WIDTH)), full((2, FILTER_WIDTH)),
                  full((FILTER_WIDTH, FILTER_WIDTH)), full((1, FILTER_WIDTH)),
                  pl.BlockSpec((None, None, FILTER_WIDTH, DH), lambda i, o: (o, i // half_tiles, 0, 0)),
                  full((1, DH))],
        out_specs=pl.BlockSpec((None, TR_F, DH), lambda i, o: (o, i, 0)),
        out_shape=jax.ShapeDtypeStruct((2, NFFT, DH), F32),
        compiler_params=_params("parallel", "parallel"),
        name="hyena_filter",
    )(zfull, w1p, b1, freq, w2, b2, w3r, dec)


TW = 2048


def _fspec_a_kernel(f_ref, h_ref, o_ref):
    o_ref[...] = _dot(f_ref[...], h_ref[...], precision=HI)


def _fspec_a_call(f_a_real, hview):
    return pl.pallas_call(
        _fspec_a_kernel,
        grid=(2, WCOL // TW),
        in_specs=[pl.BlockSpec((2 * R, R), lambda o, j: (0, 0)), pl.BlockSpec((None, R, TW), lambda o, j: (o, 0, j))],
        out_specs=pl.BlockSpec((None, 2 * R, TW), lambda o, j: (o, 0, j)),
        out_shape=jax.ShapeDtypeStruct((2, 2 * R, WCOL), F32),
        compiler_params=_params("parallel", "parallel"),
        name="hyena_fspec_a",
    )(f_a_real, hview)


def _fspec_c_kernel(m_ref, a_ref, o_ref):
    a = jnp.concatenate([a_ref[0], a_ref[1]], axis=0)
    o_ref[...] = _dot(m_ref[...], a, precision=HI) * (1.0 / NFFT)


def _fspec_c_call(m_fwd, aview):
    return pl.pallas_call(
        _fspec_c_kernel,
        grid=(R, 2),
        in_specs=[pl.BlockSpec((None, 2 * R, 2 * R), lambda k, o: (k, 0, 0)),
                  pl.BlockSpec((None, 2, None, R, DH), lambda k, o: (o, 0, k, 0, 0))],
        out_specs=pl.BlockSpec((None, None, 2 * R, DH), lambda k, o: (o, k, 0, 0)),
        out_shape=jax.ShapeDtypeStruct((2, R, 2 * R, DH), F32),
        compiler_params=_params("parallel", "parallel"),
        name="hyena_fspec_c",
    )(m_fwd, aview)


def _conv_a_kernel(f_ref, z_ref, o_ref):
    zz = jnp.concatenate([z_ref[0], z_ref[1]], axis=0).astype(BF16)
    o_ref[...] = _dot(f_ref[...], zz).astype(BF16)


def _conv_a_call(f_a, zview):
    return pl.pallas_call(
        _conv_a_kernel,
        grid=(2, WCOL // TW),
        in_specs=[pl.BlockSpec((2 * R, R), lambda p, j: (0, 0)),
                  pl.BlockSpec((2, None, HALF, TW), lambda p, j: (0, p, 0, j))],
        out_specs=pl.BlockSpec((None, 2 * R, TW), lambda p, j: (p, 0, j)),
        out_shape=jax.ShapeDtypeStruct((2, 2 * R, WCOL), BF16),
        compiler_params=_params("parallel", "parallel"),
        name="hyena_conv_a",
    )(f_a, zview)


def _conv_c_kernel(mf_ref, mi_ref, h_ref, a_ref, o_ref):
    a = jnp.concatenate([a_ref[0], a_ref[1]], axis=0)
    x = _dot(mf_ref[...], a)
    xr, xi = x[0:R], x[R:2 * R]
    hr, hi = h_ref[0:R, :], h_ref[R:2 * R, :]
    y = jnp.concatenate([xr * hr - xi * hi, xr * hi + xi * hr], axis=0).astype(BF16)
    b = _dot(mi_ref[...], y)
    o_ref[0] = b[0:R].astype(BF16)
    o_ref[1] = b[R:2 * R].astype(BF16)


def _conv_c_call(m_fwd, m_inv, hspec, order, aview):
    dspec = pl.BlockSpec((None, 2, None, R, DH), lambda k, p: (p, 0, k, 0, 0))
    mspec = pl.BlockSpec((None, 2 * R, 2 * R), lambda k, p: (k, 0, 0))
    return pl.pallas_call(
        _conv_c_kernel,
        grid=(R, 2),
        in_specs=[mspec, mspec, pl.BlockSpec((None, None, 2 * R, DH), lambda k, p: (order, k, 0, 0)), dspec],
        out_specs=dspec,
        out_shape=jax.ShapeDtypeStruct((2, 2, R, R, DH), BF16),
        compiler_params=_params("parallel", "parallel"),
        name="hyena_conv_c",
    )(m_fwd, m_inv, hspec, aview)


def _conv_out_kernel(g_ref, b_ref, z_ref, gate_ref, d_ref, o_ref):
    y = _dot(g_ref[...], b_ref[...])
    d = d_ref[...]
    o_ref[0] = gate_ref[0] * (y[0:HALF] + d * z_ref[0])
    o_ref[1] = gate_ref[1] * (y[HALF:2 * HALF] + d * z_ref[1])


def _conv_out_call(g_a, bview, zview, gateview, drow):
    bspec = pl.BlockSpec((2, None, HALF, TW), lambda p, j: (0, p, 0, j))
    return pl.pallas_call(
        _conv_out_kernel,
        grid=(2, WCOL // TW),
        in_specs=[pl.BlockSpec((R, 2 * R), lambda p, j: (0, 0)),
                  pl.BlockSpec((None, 2 * R, TW), lambda p, j: (p, 0, j)),
                  bspec, bspec, pl.BlockSpec((1, TW), lambda p, j: (0, 0))],
        out_specs=bspec,
        out_shape=jax.ShapeDtypeStruct((2, 2, HALF, WCOL), F32),
        compiler_params=_params("parallel", "parallel"),
        name="hyena_conv_out",
    )(g_a, bview, zview, gateview, drow)


TM_OUT = 512


def _outproj_kernel(x_ref, ya_ref, yb_ref, wa_ref, wb_ref, g1_ref, o_ref):
    mix = _dot(ya_ref[...].astype(BF16), wa_ref[...]) + _dot(yb_ref[...], wb_ref[...])
    o_ref[...] = x_ref[...] + g1_ref[...] * mix


def _outproj_call(x2, ya, yb, wa, wb, g1):
    tm = TM_OUT
    tiles = S // tm
    row = lambda i: (i, 0)
    return pl.pallas_call(
        _outproj_kernel,
        grid=(T // tm,),
        in_specs=[pl.BlockSpec((tm, D), row), pl.BlockSpec((tm, DH), row), pl.BlockSpec((tm, DD), row),
                  pl.BlockSpec((DH, D), lambda i: (0, 0)), pl.BlockSpec((DD, D), lambda i: (0, 0)),
                  pl.BlockSpec((None, 1, D), lambda i: (i // tiles, 0, 0))],
        out_specs=pl.BlockSpec((tm, D), row),
        out_shape=jax.ShapeDtypeStruct((T, D), F32),
        compiler_params=_params("parallel"),
        name="outproj",
    )(x2, ya, yb, wa, wb, g1)


TM_FFN = 512
FH = 16
CK = 256
assert D_FF % CK == 0


def _ffn_kernel(final, xp_ref, xm_ref, xn_ref, g_ref, sh_ref, sc_ref, g2_ref, wup_ref, cw_ref, cb_ref, wdn_ref,
                fg_ref, o_ref, hn_ref, ug_ref, uu_ref, acc_ref):
    i = pl.program_id(0)
    tiles = S // TM_FFN
    first = (i % tiles) == 0
    last = (i % tiles) == tiles - 1
    g, sh, sc = g_ref[...], sh_ref[...], sc_ref[...]
    xm = xm_ref[...]
    hn_ref[0:FH, :] = jnp.where(first, 0.0, _rms_mod(xp_ref[...], g, sh, sc)).astype(BF16)
    hn_ref[FH:FH + TM_FFN, :] = _rms_mod(xm, g, sh, sc).astype(BF16)
    hn_ref[FH + TM_FFN:2 * FH + TM_FFN, :] = jnp.where(last, 0.0, _rms_mod(xn_ref[...], g, sh, sc)).astype(BF16)
    hn = hn_ref[...]

    def conv(u_ref, c0):
        w = cw_ref[:, c0:c0 + CK]
        return (w[0:1] * u_ref[FH - 1:FH - 1 + TM_FFN, :] + w[1:2] * u_ref[FH:FH + TM_FFN, :]
                + w[2:3] * u_ref[FH + 1:FH + 1 + TM_FFN, :] + cb_ref[:, c0:c0 + CK])

    for c in range(D_FF // CK):
        c0 = c * CK
        ug_ref[...] = _dot(hn, wup_ref[:, c0:c0 + CK])
        uu_ref[...] = _dot(hn, wup_ref[:, D_FF + c0:D_FF + c0 + CK])
        act = (_silu(conv(ug_ref, c0)) * conv(uu_ref, D_FF + c0)).astype(BF16)
        part = _dot(act, wdn_ref[c0:c0 + CK, :])
        if c == 0:
            acc_ref[...] = part
        else:
            acc_ref[...] += part
    out = xm + g2_ref[...] * acc_ref[...]
    if final:
        out = (out * lax.rsqrt(jnp.mean(out * out, axis=-1, keepdims=True) + EPS)) * fg_ref[...]
    o_ref[...] = out


def _ffn_call(x2, g, sh, sc, g2, wup, cw, cb, wdn, fg, final):
    tm = TM_FFN
    tiles = S // tm
    nh = tm // FH
    row = lambda i: (i, 0)
    bvec = pl.BlockSpec((None, 1, D), lambda i: (i // tiles, 0, 0))
    full = lambda shape: pl.BlockSpec(shape, lambda i: (0,) * len(shape))
    return pl.pallas_call(
        functools.partial(_ffn_kernel, final),
        grid=(T // tm,),
        in_specs=[
            pl.BlockSpec((FH, D), lambda i: (jnp.maximum(i * nh - 1, 0), 0)),
            pl.BlockSpec((tm, D), row),
            pl.BlockSpec((FH, D), lambda i: (jnp.minimum((i + 1) * nh, T // FH - 1), 0)),
            full((1, D)), bvec, bvec, bvec,
            full((D, 2 * D_FF)), full((3, 2 * D_FF)), full((1, 2 * D_FF)), full((D_FF, D)), full((1, D)),
        ],
        out_specs=pl.BlockSpec((tm, D), row),
        out_shape=jax.ShapeDtypeStruct((T, D), F32),
        scratch_shapes=[pltpu.VMEM((tm + 2 * FH, D), BF16), pltpu.VMEM((tm + 2 * FH, CK), F32),
                        pltpu.VMEM((tm + 2 * FH, CK), F32), pltpu.VMEM((tm, D), F32)],
        compiler_params=_params("parallel"),
        name="convffn_final" if final else "convffn",
    )(x2, x2, x2, g, sh, sc, g2, wup, cw, cb, wdn, fg)


TM_SGU = 512


def _sgu_kernel(x_ref, g_ref, sh_ref, sc_ref, g1_ref, win_ref, bin_ref, lng_ref, lnb_ref, ws_ref, bs_ref, wout_ref,
                o_ref, s_ref):
    x = x_ref[...]
    hn = _rms_mod(x, g_ref[...], sh_ref[...], sc_ref[...]).astype(BF16)
    pre = _dot(hn, win_ref[...]) + bin_ref[...]
    act = 0.5 * pre * (1.0 + lax.erf(pre * (2.0 ** -0.5)))
    u = act[:, 0:D]
    v = act[:, D:2 * D]
    mu = jnp.mean(v, axis=-1, keepdims=True)
    vc = v - mu
    v = (vc * lax.rsqrt(jnp.mean(vc * vc, axis=-1, keepdims=True) + EPS)) * lng_ref[...] + lnb_ref[...]
    vb = v.astype(BF16)
    cg = D // GROUPS
    for n in range(TM_SGU // CHUNK):
        for gi in range(GROUPS):
            s_ref[n * CHUNK:(n + 1) * CHUNK, gi * cg:(gi + 1) * cg] = (
                _dot(ws_ref[gi], vb[n * CHUNK:(n + 1) * CHUNK, gi * cg:(gi + 1) * cg])
                + bs_ref[:, gi * cg:(gi + 1) * cg])
    gated = (u * s_ref[...]).astype(BF16)
    o_ref[...] = x + g1_ref[...] * _dot(gated, wout_ref[...])


def _sgu_call(x2, g, sh, sc, g1, win, bin_, lng, lnb, ws, bs_full, wout):
    tm = TM_SGU
    tiles = S // tm
    row = lambda i: (i, 0)
    bvec = pl.BlockSpec((None, 1, D), lambda i: (i // tiles, 0, 0))
    full = lambda shape: pl.BlockSpec(shape, lambda i: (0,) * len(shape))
    return pl.pallas_call(
        _sgu_kernel,
        grid=(T // tm,),
        in_specs=[pl.BlockSpec((tm, D), row), full((1, D)), bvec, bvec, bvec,
                  full((D, 2 * D)), full((1, 2 * D)), full((1, D)), full((1, D)),
                  full((GROUPS, CHUNK, CHUNK)), full((CHUNK, D)), full((D, D))],
        out_specs=pl.BlockSpec((tm, D), row),
        out_shape=jax.ShapeDtypeStruct((T, D), F32),
        scratch_shapes=[pltpu.VMEM((tm, D), F32)],
        compiler_params=_params("parallel"),
        name="sgu",
    )(x2, g, sh, sc, g1, win, bin_, lng, lnb, ws, bs_full, wout)


def kernel(x, c, ctx, c_ctx, mod_w, mod_b, norm_mix_g, norm_ffn_g, ffn_w_up, ffn_conv_w, ffn_conv_b, ffn_w_down,
           ab_w_in, hy_conv_w, hy_conv_b, hy_w1, hy_b1, hy_freq, hy_w2, hy_b2, hy_w3, hy_bias, da_lambda,
           da_subln_g, ab_w_out, sgu_w_in, sgu_b_in, sgu_ln_g, sgu_ln_b, sgu_w_s, sgu_b_s, sgu_w_out,
           final_norm_g):
    x2 = x.reshape(T, D)
    cc = jnp.concatenate([c, c_ctx[None, :], jnp.zeros((3, D), F32)], axis=0)
    mods = _mod_call(cc, mod_w, mod_b)

    def mod_vec(layer, k):
        return mods[layer, 0:B, k * D:(k + 1) * D].reshape(B, 1, D)

    w_in = ab_w_in[0].astype(BF16)
    n_hq = 3 * DH + DD
    cos_t, sin_t = _rope_tables()
    w_k = w_in[:, n_hq:n_hq + DD]
    w_vt = w_in[:, n_hq + DD:].T
    hv, hx1, hx2, q, k, vt = _inproj_call(
        x2, norm_mix_g[0][None, :], mod_vec(0, 0), mod_vec(0, 1),
        w_in[:, 0:3 * DH], jnp.concatenate([w_in[:, 3 * DH:n_hq], w_k], axis=1), w_vt,
        cos_t, sin_t, hy_conv_w[0], hy_conv_b[0][None, :])
    kc, vct = _ctxkv_call(ctx.reshape(B * CTX, D), norm_mix_g[0][None, :], mods[0, 4:5, 0:D], mods[0, 4:5, D:2 * D],
                          w_k, w_vt)
    yb = _attn_call(da_lambda[0], da_subln_g[0][:, None], q.reshape(B, S, DD), k.reshape(B, S, DD), vt,
                    kc.reshape(B, CTX, DD), vct)

    f_a, f_a_real, g_a, m_fwd, m_inv = _dft_tables()
    w1p = jnp.pad(hy_w1[0], ((0, FILTER_EMB_PAD - FILTER_EMB), (0, 0)))
    w3r = hy_w3[0].reshape(FILTER_WIDTH, 2, 2, DH).transpose(1, 2, 0, 3)
    filt = _filter_call(_filter_positions(), w1p, hy_b1[0][None, :], hy_freq[0], hy_w2[0], hy_b2[0][None, :], w3r,
                        _decay_rates())
    fa = _fspec_a_call(jnp.asarray(f_a_real), filt.reshape(2, R, WCOL))
    hspec = _fspec_c_call(jnp.asarray(m_fwd), fa.reshape(2, 2, R, R, DH))

    f_a_b, g_a_b = jnp.asarray(f_a).astype(BF16), jnp.asarray(g_a).astype(BF16)
    m_fwd_b, m_inv_b = jnp.asarray(m_fwd).astype(BF16), jnp.asarray(m_inv).astype(BF16)
    z = hv.reshape(2, 2, HALF, WCOL)
    for order, gate in enumerate((hx1, hx2)):
        a = _conv_a_call(f_a_b, z)
        bq = _conv_c_call(m_fwd_b, m_inv_b, hspec, order, a.reshape(2, 2, R, R, DH))
        drow = jnp.tile(hy_bias[0, order], TW // DH)[None, :]
        z = _conv_out_call(g_a_b, bq.reshape(2, 2 * R, WCOL), z, gate.reshape(2, 2, HALF, WCOL), drow)
    ya = z.reshape(T, DH)

    w_out = ab_w_out[0].astype(BF16)
    x2 = _outproj_call(x2, ya, yb.reshape(T, DD), w_out[0:DH], w_out[DH:], mod_vec(0, 2))
    x2 = _ffn_call(x2, norm_ffn_g[0][None, :], mod_vec(0, 3), mod_vec(0, 4), mod_vec(0, 5),
                   ffn_w_up[0].astype(BF16), ffn_conv_w[0], ffn_conv_b[0][None, :], ffn_w_down[0].astype(BF16),
                   final_norm_g[None, :], False)

    bs_full = jnp.repeat(sgu_b_s[0].T, D // GROUPS, axis=1)
    x2 = _sgu_call(x2, norm_mix_g[1][None, :], mod_vec(1, 0), mod_vec(1, 1), mod_vec(1, 2),
                   sgu_w_in[0].astype(BF16), sgu_b_in[0][None, :], sgu_ln_g[0][None, :], sgu_ln_b[0][None, :],
                   sgu_w_s[0].astype(BF16), bs_full, sgu_w_out[0].astype(BF16))
    x2 = _ffn_call(x2, norm_ffn_g[1][None, :], mod_vec(1, 3), mod_vec(1, 4), mod_vec(1, 5),
                   ffn_w_up[1].astype(BF16), ffn_conv_w[1], ffn_conv_b[1][None, :], ffn_w_down[1].astype(BF16),
                   final_norm_g[None, :], True)
    return x2.reshape(B, S, D)
```

```python
import functools
import math

import numpy as np
import jax
import jax.numpy as jnp
from jax import lax
from jax.experimental import pallas as pl
from jax.experimental.pallas import tpu as pltpu

D = 1024
B = 4
S = 8192
T = B * S
GRID_W = 64
CTX = 256
EPS = 1e-6
DH = 512
DD = 512
HEADS = 4
FILTER_EMB = 33
FILTER_EMB_PAD = 64
FILTER_WIDTH = 64
D_FF = 2816
CHUNK = 128
GROUPS = 8

NFFT = 2 * S
R = 128
HALF = R // 2
WCOL = R * DH

F32 = jnp.float32
BF16 = jnp.bfloat16
HI = lax.Precision.HIGHEST

VMEM_LIMIT = 56 * 1024 * 1024


def _params(*sem):
    return pltpu.CompilerParams(dimension_semantics=sem, vmem_limit_bytes=VMEM_LIMIT)


def _rms_mod(x, g, sh, sc):
    y = x * lax.rsqrt(jnp.mean(x * x, axis=-1, keepdims=True) + EPS)
    return (y * g) * (1.0 + sc) + sh


def _silu(x):
    return x * (1.0 / (1.0 + jnp.exp(-x)))


def _dot(a, b, **kw):
    return jnp.dot(a, b, preferred_element_type=F32, **kw)


def _mod_kernel(c_ref, w_ref, b_ref, o_ref):
    o_ref[...] = _dot(_silu(c_ref[...]), w_ref[...], precision=HI) + b_ref[...]


def _mod_call(cc, mod_w, mod_b):
    depth = mod_w.shape[0]
    tn = 1024
    return pl.pallas_call(
        _mod_kernel,
        grid=(depth, 6 * D // tn),
        in_specs=[
            pl.BlockSpec((8, D), lambda l, j: (0, 0)),
            pl.BlockSpec((None, D, tn), lambda l, j: (l, 0, j)),
            pl.BlockSpec((None, 1, tn), lambda l, j: (l, 0, j)),
        ],
        out_specs=pl.BlockSpec((None, 8, tn), lambda l, j: (l, 0, j)),
        out_shape=jax.ShapeDtypeStruct((depth, 8, 6 * D), F32),
        compiler_params=_params("parallel", "parallel"),
        name="mod",
    )(cc, mod_w, mod_b.reshape(depth, 1, 6 * D))


TM_IN = 512
HALO = 8


def _dot_nt(a, b):
    return lax.dot_general(a, b, (((1,), (1,)), ((), ())), preferred_element_type=F32)


def _inproj_kernel(xp_ref, xm_ref, xn_ref, g_ref, sh_ref, sc_ref, why_ref, wqk_ref, wvt_ref, cos_ref, sin_ref,
                   cw_ref, cb_ref, hv_ref, hx1_ref, hx2_ref, q_ref, k_ref, vt_ref, buf_ref):
    i = pl.program_id(0)
    tiles = S // TM_IN
    first = (i % tiles) == 0
    last = (i % tiles) == tiles - 1
    g, sh, sc = g_ref[...], sh_ref[...], sc_ref[...]
    hn = _rms_mod(xm_ref[...], g, sh, sc).astype(BF16)
    hp = jnp.where(first, 0.0, _rms_mod(xp_ref[...], g, sh, sc)).astype(BF16)
    hx = jnp.where(last, 0.0, _rms_mod(xn_ref[...], g, sh, sc)).astype(BF16)
    why = why_ref[...]
    buf_ref[0:HALO, :] = _dot(hp, why)
    buf_ref[HALO:HALO + TM_IN, :] = _dot(hn, why)
    buf_ref[HALO + TM_IN:2 * HALO + TM_IN, :] = _dot(hx, why)
    cw = cw_ref[...]
    conv = (cw[0:1] * buf_ref[HALO - 1:HALO - 1 + TM_IN, :] + cw[1:2] * buf_ref[HALO:HALO + TM_IN, :]
            + cw[2:3] * buf_ref[HALO + 1:HALO + 1 + TM_IN, :] + cb_ref[...])
    hv_ref[...] = conv[:, 0:DH]
    hx1_ref[...] = conv[:, DH:2 * DH]
    hx2_ref[...] = conv[:, 2 * DH:3 * DH]

    qkv = _dot(hn, wqk_ref[...])
    vt_ref[...] = _dot_nt(wvt_ref[...], hn).astype(BF16)
    cos, sin = cos_ref[...], sin_ref[...]
    lane = lax.broadcasted_iota(jnp.int32, (TM_IN, 128), 1)
    even = ((lane // 16) % 2) == 0

    def rope(xg):
        partner = jnp.where(even, pltpu.roll(xg, 112, 1), pltpu.roll(xg, 16, 1))
        return xg * cos + partner * sin

    scale = 64 ** -0.5 * math.log2(math.e)
    for h in range(HEADS):
        q_ref[:, h * 128:(h + 1) * 128] = (rope(qkv[:, h * 128:(h + 1) * 128]) * scale).astype(BF16)
        k_ref[:, h * 128:(h + 1) * 128] = rope(qkv[:, DD + h * 128:DD + (h + 1) * 128]).astype(BF16)


def _inproj_call(x2, g, sh, sc, why, wqk, wvt, cos_t, sin_t, cw, cb):
    tm = TM_IN
    tiles = S // tm
    nh = tm // HALO
    row = lambda i: (i, 0)
    bvec = pl.BlockSpec((None, 1, D), lambda i: (i // tiles, 0, 0))
    full = lambda shape: pl.BlockSpec(shape, lambda i: (0,) * len(shape))
    return pl.pallas_call(
        _inproj_kernel,
        grid=(T // tm,),
        in_specs=[
            pl.BlockSpec((HALO, D), lambda i: (jnp.maximum(i * nh - 1, 0), 0)),
            pl.BlockSpec((tm, D), row),
            pl.BlockSpec((HALO, D), lambda i: (jnp.minimum((i + 1) * nh, T // HALO - 1), 0)),
            full((1, D)), bvec, bvec,
            full((D, 3 * DH)), full((D, 2 * DD)), full((DD, D)),
            pl.BlockSpec((tm, 128), lambda i: (i % tiles, 0)),
            pl.BlockSpec((tm, 128), lambda i: (i % tiles, 0)),
            full((3, 3 * DH)), full((1, 3 * DH)),
        ],
        out_specs=[pl.BlockSpec((tm, DH), row)] * 3 + [pl.BlockSpec((tm, DD), row)] * 2
        + [pl.BlockSpec((None, DD, tm), lambda i: (i // tiles, 0, i % tiles))],
        out_shape=[jax.ShapeDtypeStruct((T, DH), F32)] * 3 + [jax.ShapeDtypeStruct((T, DD), BF16)] * 2
        + [jax.ShapeDtypeStruct((B, DD, S), BF16)],
        scratch_shapes=[pltpu.VMEM((tm + 2 * HALO, 3 * DH), F32)],
        compiler_params=_params("parallel"),
        name="inproj",
    )(x2, x2, x2, g, sh, sc, why, wqk, wvt, cos_t, sin_t, cw, cb)


def _ctxkv_kernel(x_ref, g_ref, sh_ref, sc_ref, wk_ref, wvt_ref, k_ref, vt_ref):
    hn = _rms_mod(x_ref[...], g_ref[...], sh_ref[...], sc_ref[...]).astype(BF16)
    k_ref[...] = _dot(hn, wk_ref[...]).astype(BF16)
    vt_ref[...] = _dot_nt(wvt_ref[...], hn).astype(BF16)


def _ctxkv_call(ctx2, g, sh, sc, wk, wvt):
    tm = CTX
    full = lambda shape: pl.BlockSpec(shape, lambda i: (0,) * len(shape))
    return pl.pallas_call(
        _ctxkv_kernel,
        grid=(B * CTX // tm,),
        in_specs=[pl.BlockSpec((tm, D), lambda i: (i, 0)), full((1, D)), full((1, D)), full((1, D)),
                  full((D, DD)), full((DD, D))],
        out_specs=[pl.BlockSpec((tm, DD), lambda i: (i, 0)), pl.BlockSpec((None, DD, tm), lambda i: (i, 0, 0))],
        out_shape=[jax.ShapeDtypeStruct((B * CTX, DD), BF16), jax.ShapeDtypeStruct((B, DD, CTX), BF16)],
        compiler_params=_params("parallel"),
        name="ctxkv",
    )(ctx2, g, sh, sc, wk, wvt)


TQ = 256
TK = 1024
ONES_ROWS = 16
LAM_INIT0 = 0.8 - 0.6 * math.exp(-0.3 * 0)


def _attn_kernel(lam_ref, g_ref, q_ref, k_ref, vt_ref, kc_ref, vct_ref, o_ref, sa_ref, sb_ref, acc_ref):
    lane = lax.broadcasted_iota(jnp.int32, (TQ, 128), 1)
    lo = lane < 64
    lp = lam_ref[...]
    lam = (jnp.exp(jnp.sum(lp[0:1] * lp[1:2], axis=1, keepdims=True))
           - jnp.exp(jnp.sum(lp[2:3] * lp[3:4], axis=1, keepdims=True)) + LAM_INIT0)
    nblk = S // TK

    def key_block(j):
        if j == 0:
            return kc_ref[...], vct_ref[...], 0, CTX
        return (k_ref[(j - 1) * TK:j * TK, :], vt_ref[:, (j - 1) * TK:j * TK], CTX + (j - 1) * TK, TK)

    def load_q(t):
        q = q_ref[pl.ds(pl.multiple_of(t * TQ, TQ), TQ), :]
        zero = jnp.zeros_like(q)
        return jnp.where(lo, q, zero), jnp.where(lo, zero, q)

    def scores_block(qs, j, s_ref, m8):
        kblk, _, off, n = key_block(j)
        out = []
        for mi in range(2):
            s = _dot_nt(kblk, qs[mi])
            s_ref[mi, off:off + n, :] = s
            part = jnp.max(s.reshape(n // 8, 8, TQ), axis=0)
            out.append(part if m8 is None else jnp.maximum(m8[mi], part))
        return out

    def probs_block(j, s_ref, m):
        _, vtblk, off, n = key_block(j)
        aug = jnp.concatenate([vtblk, jnp.ones((ONES_ROWS, n), BF16)], axis=0)
        for mi in range(2):
            p = jnp.exp2(s_ref[mi, off:off + n, :] - m[mi]).astype(BF16)
            pv = _dot(aug, p)
            if j == 0:
                acc_ref[mi] = pv
            else:
                acc_ref[mi] += pv

    def finish(t):
        o = (acc_ref[0, 0:128, :] / acc_ref[0, 128:129, :]
             - lam * (acc_ref[1, 0:128, :] / acc_ref[1, 128:129, :]))
        o = o * lax.rsqrt(jnp.mean(o * o, axis=0, keepdims=True) + EPS)
        o = (o * g_ref[...]) * (1.0 - LAM_INIT0)
        o_ref[pl.ds(pl.multiple_of(t * TQ, TQ), TQ), :] = o.T.astype(o_ref.dtype)

    def col_max(m8):
        return [jnp.max(m8[mi], axis=0, keepdims=True) for mi in range(2)]

    def stage(t_score, s_write, t_prob, s_read, m_read):
        qs = load_q(t_score)
        m8 = None
        for j in range(nblk + 1):
            m8 = scores_block(qs, j, s_write, m8)
            probs_block(j, s_read, m_read)
        finish(t_prob)
        return col_max(m8)

    qs0 = load_q(0)
    m8 = None
    for j in range(nblk + 1):
        m8 = scores_block(qs0, j, sa_ref, m8)
    m_a0 = col_max(m8)

    ntile = S // TQ

    def body(i, m_a):
        t0 = 2 * i
        m_b = stage(t0 + 1, sb_ref, t0, sa_ref, m_a)
        return tuple(stage(jnp.minimum(t0 + 2, ntile - 1), sa_ref, t0 + 1, sb_ref, m_b))

    lax.fori_loop(0, ntile // 2, body, tuple(m_a0))


def _attn_call(lam_p, subln_g_col, q, k, vt, kc, vct):
    qspec = pl.BlockSpec((None, S, 128), lambda b, h: (b, 0, h))
    return pl.pallas_call(
        _attn_kernel,
        grid=(B, HEADS),
        in_specs=[pl.BlockSpec((4, 64), lambda b, h: (0, 0)), pl.BlockSpec((128, 1), lambda b, h: (0, 0)),
                  qspec, qspec,
                  pl.BlockSpec((None, 128, S), lambda b, h: (b, h, 0)),
                  pl.BlockSpec((None, CTX, 128), lambda b, h: (b, 0, h)),
                  pl.BlockSpec((None, 128, CTX), lambda b, h: (b, h, 0))],
        out_specs=qspec,
        out_shape=jax.ShapeDtypeStruct((B, S, DD), BF16),
        scratch_shapes=[pltpu.VMEM((2, CTX + S, TQ), F32), pltpu.VMEM((2, CTX + S, TQ), F32),
                        pltpu.VMEM((2, 128 + ONES_ROWS, TQ), F32)],
        compiler_params=_params("parallel", "parallel"),
        name="diffattn",
    )(lam_p, subln_g_col, q, k, vt, kc, vct)


def _cplx_block(m):
    return np.block([[m.real, -m.imag], [m.imag, m.real]])


@functools.lru_cache(maxsize=None)
def _dft_tables():
    k = np.arange(R)
    w_r = np.exp(-2j * np.pi * np.outer(k, k) / R)
    f_a = _cplx_block(w_r[:, :HALF])
    f_a_real = np.concatenate([w_r.real, w_r.imag], axis=0)
    g_a = _cplx_block(np.conj(w_r)[:HALF, :])
    tw = np.exp(-2j * np.pi * np.outer(k, k) / NFFT)
    m_f = w_r[None, :, :] * tw[:, None, :]
    m_fwd = np.stack([_cplx_block(m_f[i]) for i in range(R)])
    m_inv = np.stack([_cplx_block(np.conj(m_f[i]).T) for i in range(R)])
    return (f_a.astype(np.float32), f_a_real.astype(np.float32), g_a.astype(np.float32),
            m_fwd.astype(np.float32), m_inv.astype(np.float32))


def _filter_positions():
    idx = np.arange(NFFT)
    pos = np.where(idx < S, idx, NFFT - idx) % S
    bands = (FILTER_EMB - 1) // 2
    posf = jnp.arange(S, dtype=F32)
    t = jnp.linspace(0.0, 1.0, S, dtype=F32)[:, None]
    ang = (2.0 * math.pi / S) * posf[:, None] * jnp.linspace(1e-4, bands - 1, bands, dtype=F32)[None, :]
    z = jnp.concatenate([t, jnp.cos(ang), -jnp.sin(ang)], axis=-1)
    z = jnp.pad(z, ((0, 0), (0, FILTER_EMB_PAD - FILTER_EMB)))
    return z[pos]


def _decay_rates():
    return jnp.abs(jnp.linspace(math.log(1e-2) / 1.5, math.log(1e-2) / 0.3, DH, dtype=F32))[None, :]


def _rope_tables():
    rows = S // GRID_W
    row = jnp.repeat(jnp.arange(rows, dtype=jnp.int32), GRID_W).astype(F32)
    col = jnp.tile(jnp.arange(GRID_W, dtype=jnp.int32), rows).astype(F32)
    m = 16
    inv = 10000.0 ** (-jnp.arange(m, dtype=F32) / m)
    ang_r = row[:, None] * inv[None, :]
    ang_c = col[:, None] * inv[None, :]
    cos = jnp.concatenate([jnp.cos(ang_r)] * 2 + [jnp.cos(ang_c)] * 2, axis=1)
    sin = jnp.concatenate([-jnp.sin(ang_r), jnp.sin(ang_r), -jnp.sin(ang_c), jnp.sin(ang_c)], axis=1)
    return jnp.concatenate([cos, cos], axis=1), jnp.concatenate([sin, sin], axis=1)


TR_F = 512


def _filter_kernel(z_ref, w1_ref, b1_ref, fr_ref, w2_ref, b2_ref, w3_ref, dec_ref, o_ref):
    i = pl.program_id(0)
    z = z_ref[...]
    fr = fr_ref[...]
    h = jnp.sin(fr[0:1] * (_dot(z, w1_ref[...], precision=HI) + b1_ref[...]))
    h = jnp.sin(fr[1:2] * (_dot(h, w2_ref[...], precision=HI) + b2_ref[...]))
    h = _dot(h, w3_ref[...], precision=HI)
    window = jnp.exp(-z[:, 0:1] * dec_ref[...]) + 0.05
    ridx = i * TR_F + lax.broadcasted_iota(jnp.int32, (TR_F, 1), 0)
    o_ref[...] = jnp.where(ridx == S, 0.0, h * window)


def _filter_call(zfull, w1p, b1, freq, w2, b2, w3r, dec):
    full = lambda shape: pl.BlockSpec(shape, lambda i, o: (0,) * len(shape))
    half_tiles = S // TR_F
    return pl.pallas_call(
        _filter_kernel,
        grid=(NFFT // TR_F, 2),
        in_specs=[pl.BlockSpec((TR_F, FILTER_EMB_PAD), lambda i, o: (i, 0)),
                  full((FILTER_EMB_PAD, FILTER_WIDTH)), full((1, FILTER_WIDTH)), full((2, FILTER_WIDTH)),
                  full((FILTER_WIDTH, FILTER_WIDTH)), full((1, FILTER_WIDTH)),
                  pl.BlockSpec((None, None, FILTER_WIDTH, DH), lambda i, o: (o, i // half_tiles, 0, 0)),
                  full((1, DH))],
        out_specs=pl.BlockSpec((None, TR_F, DH), lambda i, o: (o, i, 0)),
        out_shape=jax.ShapeDtypeStruct((2, NFFT, DH), F32),
        compiler_params=_params("parallel", "parallel"),
        name="hyena_filter",
    )(zfull, w1p, b1, freq, w2, b2, w3r, dec)


TW = 2048


def _fspec_a_kernel(f_ref, h_ref, o_ref):
    o_ref[...] = _dot(f_ref[...], h_ref[...], precision=HI)


def _fspec_a_call(f_a_real, hview):
    return pl.pallas_call(
        _fspec_a_kernel,
        grid=(2, WCOL // TW),
        in_specs=[pl.BlockSpec((2 * R, R), lambda o, j: (0, 0)), pl.BlockSpec((None, R, TW), lambda o, j: (o, 0, j))],
        out_specs=pl.BlockSpec((None, 2 * R, TW), lambda o, j: (o, 0, j)),
        out_shape=jax.ShapeDtypeStruct((2, 2 * R, WCOL), F32),
        compiler_params=_params("parallel", "parallel"),
        name="hyena_fspec_a",
    )(f_a_real, hview)


def _fspec_c_kernel(m_ref, a_ref, o_ref):
    a = jnp.concatenate([a_ref[0], a_ref[1]], axis=0)
    o_ref[...] = _dot(m_ref[...], a, precision=HI) * (1.0 / NFFT)


def _fspec_c_call(m_fwd, aview):
    return pl.pallas_call(
        _fspec_c_kernel,
        grid=(R, 2),
        in_specs=[pl.BlockSpec((None, 2 * R, 2 * R), lambda k, o: (k, 0, 0)),
                  pl.BlockSpec((None, 2, None, R, DH), lambda k, o: (o, 0, k, 0, 0))],
        out_specs=pl.BlockSpec((None, None, 2 * R, DH), lambda k, o: (o, k, 0, 0)),
        out_shape=jax.ShapeDtypeStruct((2, R, 2 * R, DH), F32),
        compiler_params=_params("parallel", "parallel"),
        name="hyena_fspec_c",
    )(m_fwd, aview)


def _conv_a_kernel(f_ref, z_ref, o_ref):
    zz = jnp.concatenate([z_ref[0], z_ref[1]], axis=0).astype(BF16)
    o_ref[...] = _dot(f_ref[...], zz).astype(BF16)


def _conv_a_call(f_a, zview):
    return pl.pallas_call(
        _conv_a_kernel,
        grid=(2, WCOL // TW),
        in_specs=[pl.BlockSpec((2 * R, R), lambda p, j: (0, 0)),
                  pl.BlockSpec((2, None, HALF, TW), lambda p, j: (0, p, 0, j))],
        out_specs=pl.BlockSpec((None, 2 * R, TW), lambda p, j: (p, 0, j)),
        out_shape=jax.ShapeDtypeStruct((2, 2 * R, WCOL), BF16),
        compiler_params=_params("parallel", "parallel"),
        name="hyena_conv_a",
    )(f_a, zview)


def _conv_c_kernel(mf_ref, mi_ref, h_ref, a_ref, o_ref):
    a = jnp.concatenate([a_ref[0], a_ref[1]], axis=0)
    x = _dot(mf_ref[...], a)
    xr, xi = x[0:R], x[R:2 * R]
    hr, hi = h_ref[0:R, :], h_ref[R:2 * R, :]
    y = jnp.concatenate([xr * hr - xi * hi, xr * hi + xi * hr], axis=0).astype(BF16)
    b = _dot(mi_ref[...], y)
    o_ref[0] = b[0:R].astype(BF16)
    o_ref[1] = b[R:2 * R].astype(BF16)


def _conv_c_call(m_fwd, m_inv, hspec, order, aview):
    dspec = pl.BlockSpec((None, 2, None, R, DH), lambda k, p: (p, 0, k, 0, 0))
    mspec = pl.BlockSpec((None, 2 * R, 2 * R), lambda k, p: (k, 0, 0))
    return pl.pallas_call(
        _conv_c_kernel,
        grid=(R, 2),
        in_specs=[mspec, mspec, pl.BlockSpec((None, None, 2 * R, DH), lambda k, p: (order, k, 0, 0)), dspec],
        out_specs=dspec,
        out_shape=jax.ShapeDtypeStruct((2, 2, R, R, DH), BF16),
        compiler_params=_params("parallel", "parallel"),
        name="hyena_conv_c",
    )(m_fwd, m_inv, hspec, aview)


def _conv_out_kernel(g_ref, b_ref, z_ref, gate_ref, d_ref, o_ref):
    y = _dot(g_ref[...], b_ref[...])
    d = d_ref[...]
    o_ref[0] = gate_ref[0] * (y[0:HALF] + d * z_ref[0])
    o_ref[1] = gate_ref[1] * (y[HALF:2 * HALF] + d * z_ref[1])


def _conv_out_call(g_a, bview, zview, gateview, drow):
    bspec = pl.BlockSpec((2, None, HALF, TW), lambda p, j: (0, p, 0, j))
    return pl.pallas_call(
        _conv_out_kernel,
        grid=(2, WCOL // TW),
        in_specs=[pl.BlockSpec((R, 2 * R), lambda p, j: (0, 0)),
                  pl.BlockSpec((None, 2 * R, TW), lambda p, j: (p, 0, j)),
                  bspec, bspec, pl.BlockSpec((1, TW), lambda p, j: (0, 0))],
        out_specs=bspec,
        out_shape=jax.ShapeDtypeStruct((2, 2, HALF, WCOL), F32),
        compiler_params=_params("parallel", "parallel"),
        name="hyena_conv_out",
    )(g_a, bview, zview, gateview, drow)


TM_OUT = 512


def _outproj_kernel(x_ref, ya_ref, yb_ref, wa_ref, wb_ref, g1_ref, o_ref):
    mix = _dot(ya_ref[...].astype(BF16), wa_ref[...]) + _dot(yb_ref[...], wb_ref[...])
    o_ref[...] = x_ref[...] + g1_ref[...] * mix


def _outproj_call(x2, ya, yb, wa, wb, g1):
    tm = TM_OUT
    tiles = S // tm
    row = lambda i: (i, 0)
    return pl.pallas_call(
        _outproj_kernel,
        grid=(T // tm,),
        in_specs=[pl.BlockSpec((tm, D), row), pl.BlockSpec((tm, DH), row), pl.BlockSpec((tm, DD), row),
                  pl.BlockSpec((DH, D), lambda i: (0, 0)), pl.BlockSpec((DD, D), lambda i: (0, 0)),
                  pl.BlockSpec((None, 1, D), lambda i: (i // tiles, 0, 0))],
        out_specs=pl.BlockSpec((tm, D), row),
        out_shape=jax.ShapeDtypeStruct((T, D), F32),
        compiler_params=_params("parallel"),
        name="outproj",
    )(x2, ya, yb, wa, wb, g1)


TM_FFN = 512
FH = 16
CK = 256
assert D_FF % CK == 0


def _ffn_kernel(final, xp_ref, xm_ref, xn_ref, g_ref, sh_ref, sc_ref, g2_ref, wup_ref, cw_ref, cb_ref, wdn_ref,
                fg_ref, o_ref, hn_ref, ug_ref, uu_ref, act_ref):
    i = pl.program_id(0)
    tiles = S // TM_FFN
    first = (i % tiles) == 0
    last = (i % tiles) == tiles - 1
    g, sh, sc = g_ref[...], sh_ref[...], sc_ref[...]
    xm = xm_ref[...]
    hn_ref[0:FH, :] = jnp.where(first, 0.0, _rms_mod(xp_ref[...], g, sh, sc)).astype(BF16)
    hn_ref[FH:FH + TM_FFN, :] = _rms_mod(xm, g, sh, sc).astype(BF16)
    hn_ref[FH + TM_FFN:2 * FH + TM_FFN, :] = jnp.where(last, 0.0, _rms_mod(xn_ref[...], g, sh, sc)).astype(BF16)
    hn = hn_ref[...]

    def conv(u_ref, c0):
        w = cw_ref[:, c0:c0 + CK]
        return (w[0:1] * u_ref[FH - 1:FH - 1 + TM_FFN, :] + w[1:2] * u_ref[FH:FH + TM_FFN, :]
                + w[2:3] * u_ref[FH + 1:FH + 1 + TM_FFN, :] + cb_ref[:, c0:c0 + CK])

    for c in range(D_FF // CK):
        c0 = c * CK
        ug, uu = ug_ref.at[c % 2], uu_ref.at[c % 2]
        ug[...] = _dot(hn, wup_ref[:, c0:c0 + CK])
        uu[...] = _dot(hn, wup_ref[:, D_FF + c0:D_FF + c0 + CK])
        act_ref[:, c0:c0 + CK] = (_silu(conv(ug, c0)) * conv(uu, D_FF + c0)).astype(BF16)
    out = xm + g2_ref[...] * _dot(act_ref[...], wdn_ref[...])
    if final:
        out = (out * lax.rsqrt(jnp.mean(out * out, axis=-1, keepdims=True) + EPS)) * fg_ref[...]
    o_ref[...] = out


def _ffn_call(x2, g, sh, sc, g2, wup, cw, cb, wdn, fg, final):
    tm = TM_FFN
    tiles = S // tm
    nh = tm // FH
    row = lambda i: (i, 0)
    bvec = pl.BlockSpec((None, 1, D), lambda i: (i // tiles, 0, 0))
    full = lambda shape: pl.BlockSpec(shape, lambda i: (0,) * len(shape))
    return pl.pallas_call(
        functools.partial(_ffn_kernel, final),
        grid=(T // tm,),
        in_specs=[
            pl.BlockSpec((FH, D), lambda i: (jnp.maximum(i * nh - 1, 0), 0)),
            pl.BlockSpec((tm, D), row),
            pl.BlockSpec((FH, D), lambda i: (jnp.minimum((i + 1) * nh, T // FH - 1), 0)),
            full((1, D)), bvec, bvec, bvec,
            full((D, 2 * D_FF)), full((3, 2 * D_FF)), full((1, 2 * D_FF)), full((D_FF, D)), full((1, D)),
        ],
        out_specs=pl.BlockSpec((tm, D), row),
        out_shape=jax.ShapeDtypeStruct((T, D), F32),
        scratch_shapes=[pltpu.VMEM((tm + 2 * FH, D), BF16), pltpu.VMEM((2, tm + 2 * FH, CK), F32),
                        pltpu.VMEM((2, tm + 2 * FH, CK), F32), pltpu.VMEM((tm, D_FF), BF16)],
        compiler_params=_params("parallel"),
        name="convffn_final" if final else "convffn",
    )(x2, x2, x2, g, sh, sc, g2, wup, cw, cb, wdn, fg)


TM_SGU = 512


def _sgu_kernel(x_ref, g_ref, sh_ref, sc_ref, g1_ref, win_ref, bin_ref, lng_ref, lnb_ref, ws_ref, bs_ref, wout_ref,
                o_ref, s_ref):
    x = x_ref[...]
    hn = _rms_mod(x, g_ref[...], sh_ref[...], sc_ref[...]).astype(BF16)
    pre = _dot(hn, win_ref[...]) + bin_ref[...]
    act = 0.5 * pre * (1.0 + lax.erf(pre * (2.0 ** -0.5)))
    u = act[:, 0:D]
    v = act[:, D:2 * D]
    mu = jnp.mean(v, axis=-1, keepdims=True)
    vc = v - mu
    v = (vc * lax.rsqrt(jnp.mean(vc * vc, axis=-1, keepdims=True) + EPS)) * lng_ref[...] + lnb_ref[...]
    vb = v.astype(BF16)
    cg = D // GROUPS
    for n in range(TM_SGU // CHUNK):
        for gi in range(GROUPS):
            s_ref[n * CHUNK:(n + 1) * CHUNK, gi * cg:(gi + 1) * cg] = (
                _dot(ws_ref[gi], vb[n * CHUNK:(n + 1) * CHUNK, gi * cg:(gi + 1) * cg])
                + bs_ref[:, gi * cg:(gi + 1) * cg])
    gated = (u * s_ref[...]).astype(BF16)
    o_ref[...] = x + g1_ref[...] * _dot(gated, wout_ref[...])


def _sgu_call(x2, g, sh, sc, g1, win, bin_, lng, lnb, ws, bs_full, wout):
    tm = TM_SGU
    tiles = S // tm
    row = lambda i: (i, 0)
    bvec = pl.BlockSpec((None, 1, D), lambda i: (i // tiles, 0, 0))
    full = lambda shape: pl.BlockSpec(shape, lambda i: (0,) * len(shape))
    return pl.pallas_call(
        _sgu_kernel,
        grid=(T // tm,),
        in_specs=[pl.BlockSpec((tm, D), row), full((1, D)), bvec, bvec, bvec,
                  full((D, 2 * D)), full((1, 2 * D)), full((1, D)), full((1, D)),
                  full((GROUPS, CHUNK, CHUNK)), full((CHUNK, D)), full((D, D))],
        out_specs=pl.BlockSpec((tm, D), row),
        out_shape=jax.ShapeDtypeStruct((T, D), F32),
        scratch_shapes=[pltpu.VMEM((tm, D), F32)],
        compiler_params=_params("parallel"),
        name="sgu",
    )(x2, g, sh, sc, g1, win, bin_, lng, lnb, ws, bs_full, wout)


def kernel(x, c, ctx, c_ctx, mod_w, mod_b, norm_mix_g, norm_ffn_g, ffn_w_up, ffn_conv_w, ffn_conv_b, ffn_w_down,
           ab_w_in, hy_conv_w, hy_conv_b, hy_w1, hy_b1, hy_freq, hy_w2, hy_b2, hy_w3, hy_bias, da_lambda,
           da_subln_g, ab_w_out, sgu_w_in, sgu_b_in, sgu_ln_g, sgu_ln_b, sgu_w_s, sgu_b_s, sgu_w_out,
           final_norm_g):
    x2 = x.reshape(T, D)
    cc = jnp.concatenate([c, c_ctx[None, :], jnp.zeros((3, D), F32)], axis=0)
    mods = _mod_call(cc, mod_w, mod_b)

    def mod_vec(layer, k):
        return mods[layer, 0:B, k * D:(k + 1) * D].reshape(B, 1, D)

    w_in = ab_w_in[0].astype(BF16)
    n_hq = 3 * DH + DD
    cos_t, sin_t = _rope_tables()
    w_k = w_in[:, n_hq:n_hq + DD]
    w_vt = w_in[:, n_hq + DD:].T
    hv, hx1, hx2, q, k, vt = _inproj_call(
        x2, norm_mix_g[0][None, :], mod_vec(0, 0), mod_vec(0, 1),
        w_in[:, 0:3 * DH], jnp.concatenate([w_in[:, 3 * DH:n_hq], w_k], axis=1), w_vt,
        cos_t, sin_t, hy_conv_w[0], hy_conv_b[0][None, :])
    kc, vct = _ctxkv_call(ctx.reshape(B * CTX, D), norm_mix_g[0][None, :], mods[0, 4:5, 0:D], mods[0, 4:5, D:2 * D],
                          w_k, w_vt)
    yb = _attn_call(da_lambda[0], da_subln_g[0][:, None], q.reshape(B, S, DD), k.reshape(B, S, DD), vt,
                    kc.reshape(B, CTX, DD), vct)

    f_a, f_a_real, g_a, m_fwd, m_inv = _dft_tables()
    w1p = jnp.pad(hy_w1[0], ((0, FILTER_EMB_PAD - FILTER_EMB), (0, 0)))
    w3r = hy_w3[0].reshape(FILTER_WIDTH, 2, 2, DH).transpose(1, 2, 0, 3)
    filt = _filter_call(_filter_positions(), w1p, hy_b1[0][None, :], hy_freq[0], hy_w2[0], hy_b2[0][None, :], w3r,
                        _decay_rates())
    fa = _fspec_a_call(jnp.asarray(f_a_real), filt.reshape(2, R, WCOL))
    hspec = _fspec_c_call(jnp.asarray(m_fwd), fa.reshape(2, 2, R, R, DH))

    f_a_b, g_a_b = jnp.asarray(f_a).astype(BF16), jnp.asarray(g_a).astype(BF16)
    m_fwd_b, m_inv_b = jnp.asarray(m_fwd).astype(BF16), jnp.asarray(m_inv).astype(BF16)
    z = hv.reshape(2, 2, HALF, WCOL)
    for order, gate in enumerate((hx1, hx2)):
        a = _conv_a_call(f_a_b, z)
        bq = _conv_c_call(m_fwd_b, m_inv_b, hspec, order, a.reshape(2, 2, R, R, DH))
        drow = jnp.tile(hy_bias[0, order], TW // DH)[None, :]
        z = _conv_out_call(g_a_b, bq.reshape(2, 2 * R, WCOL), z, gate.reshape(2, 2, HALF, WCOL), drow)
    ya = z.reshape(T, DH)

    w_out = ab_w_out[0].astype(BF16)
    x2 = _outproj_call(x2, ya, yb.reshape(T, DD), w_out[0:DH], w_out[DH:], mod_vec(0, 2))
    x2 = _ffn_call(x2, norm_ffn_g[0][None, :], mod_vec(0, 3), mod_vec(0, 4), mod_vec(0, 5),
                   ffn_w_up[0].astype(BF16), ffn_conv_w[0], ffn_conv_b[0][None, :], ffn_w_down[0].astype(BF16),
                   final_norm_g[None, :], False)

    bs_full = jnp.repeat(sgu_b_s[0].T, D // GROUPS, axis=1)
    x2 = _sgu_call(x2, norm_mix_g[1][None, :], mod_vec(1, 0), mod_vec(1, 1), mod_vec(1, 2),
                   sgu_w_in[0].astype(BF16), sgu_b_in[0][None, :], sgu_ln_g[0][None, :], sgu_ln_b[0][None, :],
                   sgu_w_s[0].astype(BF16), bs_full, sgu_w_out[0].astype(BF16))
    x2 = _ffn_call(x2, norm_ffn_g[1][None, :], mod_vec(1, 3), mod_vec(1, 4), mod_vec(1, 5),
                   ffn_w_up[1].astype(BF16), ffn_conv_w[1], ffn_conv_b[1][None, :], ffn_w_down[1].astype(BF16),
                   final_norm_g[None, :], True)
    return x2.reshape(B, S, D)
```

```python
import functools
import math

import numpy as np
import jax
import jax.numpy as jnp
from jax import lax
from jax.experimental import pallas as pl
from jax.experimental.pallas import tpu as pltpu

D = 1024
B = 4
S = 8192
T = B * S
GRID_W = 64
CTX = 256
EPS = 1e-6
DH = 512
DD = 512
HEADS = 4
FILTER_EMB = 33
FILTER_EMB_PAD = 64
FILTER_WIDTH = 64
D_FF = 2816
CHUNK = 128
GROUPS = 8

NFFT = 2 * S
R = 128
HALF = R // 2
WCOL = R * DH

F32 = jnp.float32
BF16 = jnp.bfloat16
HI = lax.Precision.HIGHEST

VMEM_LIMIT = 56 * 1024 * 1024


def _params(*sem):
    return pltpu.CompilerParams(dimension_semantics=sem, vmem_limit_bytes=VMEM_LIMIT)


def _rms_mod(x, g, sh, sc):
    y = x * lax.rsqrt(jnp.mean(x * x, axis=-1, keepdims=True) + EPS)
    return (y * g) * (1.0 + sc) + sh


def _silu(x):
    return x * (1.0 / (1.0 + jnp.exp(-x)))


def _dot(a, b, **kw):
    return jnp.dot(a, b, preferred_element_type=F32, **kw)


def _mod_kernel(c_ref, w_ref, b_ref, o_ref):
    o_ref[...] = _dot(_silu(c_ref[...]), w_ref[...], precision=HI) + b_ref[...]


def _mod_call(cc, mod_w, mod_b):
    depth = mod_w.shape[0]
    tn = 1024
    return pl.pallas_call(
        _mod_kernel,
        grid=(depth, 6 * D // tn),
        in_specs=[
            pl.BlockSpec((8, D), lambda l, j: (0, 0)),
            pl.BlockSpec((None, D, tn), lambda l, j: (l, 0, j)),
            pl.BlockSpec((None, 1, tn), lambda l, j: (l, 0, j)),
        ],
        out_specs=pl.BlockSpec((None, 8, tn), lambda l, j: (l, 0, j)),
        out_shape=jax.ShapeDtypeStruct((depth, 8, 6 * D), F32),
        compiler_params=_params("parallel", "parallel"),
        name="mod",
    )(cc, mod_w, mod_b.reshape(depth, 1, 6 * D))


TM_IN = 512
HALO = 8


def _dot_nt(a, b):
    return lax.dot_general(a, b, (((1,), (1,)), ((), ())), preferred_element_type=F32)


def _inproj_kernel(xp_ref, xm_ref, xn_ref, g_ref, sh_ref, sc_ref, why_ref, wqk_ref, wvt_ref, cos_ref, sin_ref,
                   cw_ref, cb_ref, hv_ref, hx1_ref, hx2_ref, q_ref, k_ref, vt_ref, buf_ref):
    i = pl.program_id(0)
    tiles = S // TM_IN
    first = (i % tiles) == 0
    last = (i % tiles) == tiles - 1
    g, sh, sc = g_ref[...], sh_ref[...], sc_ref[...]
    hn = _rms_mod(xm_ref[...], g, sh, sc).astype(BF16)
    hp = jnp.where(first, 0.0, _rms_mod(xp_ref[...], g, sh, sc)).astype(BF16)
    hx = jnp.where(last, 0.0, _rms_mod(xn_ref[...], g, sh, sc)).astype(BF16)
    why = why_ref[...]
    buf_ref[0:HALO, :] = _dot(hp, why)
    buf_ref[HALO:HALO + TM_IN, :] = _dot(hn, why)
    buf_ref[HALO + TM_IN:2 * HALO + TM_IN, :] = _dot(hx, why)
    cw = cw_ref[...]
    conv = (cw[0:1] * buf_ref[HALO - 1:HALO - 1 + TM_IN, :] + cw[1:2] * buf_ref[HALO:HALO + TM_IN, :]
            + cw[2:3] * buf_ref[HALO + 1:HALO + 1 + TM_IN, :] + cb_ref[...])
    hv_ref[...] = conv[:, 0:DH]
    hx1_ref[...] = conv[:, DH:2 * DH]
    hx2_ref[...] = conv[:, 2 * DH:3 * DH]

    qkv = _dot(hn, wqk_ref[...])
    vt_ref[...] = _dot_nt(wvt_ref[...], hn).astype(BF16)
    cos, sin = cos_ref[...], sin_ref[...]
    lane = lax.broadcasted_iota(jnp.int32, (TM_IN, 128), 1)
    even = ((lane // 16) % 2) == 0

    def rope(xg):
        partner = jnp.where(even, pltpu.roll(xg, 112, 1), pltpu.roll(xg, 16, 1))
        return xg * cos + partner * sin

    scale = 64 ** -0.5 * math.log2(math.e)
    for h in range(HEADS):
        q_ref[:, h * 128:(h + 1) * 128] = (rope(qkv[:, h * 128:(h + 1) * 128]) * scale).astype(BF16)
        k_ref[:, h * 128:(h + 1) * 128] = rope(qkv[:, DD + h * 128:DD + (h + 1) * 128]).astype(BF16)


def _inproj_call(x2, g, sh, sc, why, wqk, wvt, cos_t, sin_t, cw, cb):
    tm = TM_IN
    tiles = S // tm
    nh = tm // HALO
    row = lambda i: (i, 0)
    bvec = pl.BlockSpec((None, 1, D), lambda i: (i // tiles, 0, 0))
    full = lambda shape: pl.BlockSpec(shape, lambda i: (0,) * len(shape))
    return pl.pallas_call(
        _inproj_kernel,
        grid=(T // tm,),
        in_specs=[
            pl.BlockSpec((HALO, D), lambda i: (jnp.maximum(i * nh - 1, 0), 0)),
            pl.BlockSpec((tm, D), row),
            pl.BlockSpec((HALO, D), lambda i: (jnp.minimum((i + 1) * nh, T // HALO - 1), 0)),
            full((1, D)), bvec, bvec,
            full((D, 3 * DH)), full((D, 2 * DD)), full((DD, D)),
            pl.BlockSpec((tm, 128), lambda i: (i % tiles, 0)),
            pl.BlockSpec((tm, 128), lambda i: (i % tiles, 0)),
            full((3, 3 * DH)), full((1, 3 * DH)),
        ],
        out_specs=[pl.BlockSpec((tm, DH), row)] * 3 + [pl.BlockSpec((tm, DD), row)] * 2
        + [pl.BlockSpec((None, DD, tm), lambda i: (i // tiles, 0, i % tiles))],
        out_shape=[jax.ShapeDtypeStruct((T, DH), F32)] * 3 + [jax.ShapeDtypeStruct((T, DD), BF16)] * 2
        + [jax.ShapeDtypeStruct((B, DD, S), BF16)],
        scratch_shapes=[pltpu.VMEM((tm + 2 * HALO, 3 * DH), F32)],
        compiler_params=_params("parallel"),
        name="inproj",
    )(x2, x2, x2, g, sh, sc, why, wqk, wvt, cos_t, sin_t, cw, cb)


def _ctxkv_kernel(x_ref, g_ref, sh_ref, sc_ref, wk_ref, wvt_ref, k_ref, vt_ref):
    hn = _rms_mod(x_ref[...], g_ref[...], sh_ref[...], sc_ref[...]).astype(BF16)
    k_ref[...] = _dot(hn, wk_ref[...]).astype(BF16)
    vt_ref[...] = _dot_nt(wvt_ref[...], hn).astype(BF16)


def _ctxkv_call(ctx2, g, sh, sc, wk, wvt):
    tm = CTX
    full = lambda shape: pl.BlockSpec(shape, lambda i: (0,) * len(shape))
    return pl.pallas_call(
        _ctxkv_kernel,
        grid=(B * CTX // tm,),
        in_specs=[pl.BlockSpec((tm, D), lambda i: (i, 0)), full((1, D)), full((1, D)), full((1, D)),
                  full((D, DD)), full((DD, D))],
        out_specs=[pl.BlockSpec((tm, DD), lambda i: (i, 0)), pl.BlockSpec((None, DD, tm), lambda i: (i, 0, 0))],
        out_shape=[jax.ShapeDtypeStruct((B * CTX, DD), BF16), jax.ShapeDtypeStruct((B, DD, CTX), BF16)],
        compiler_params=_params("parallel"),
        name="ctxkv",
    )(ctx2, g, sh, sc, wk, wvt)


TQ = 256
TK = 1024
ONES_ROWS = 16
LAM_INIT0 = 0.8 - 0.6 * math.exp(-0.3 * 0)


def _attn_kernel(lam_ref, g_ref, q_ref, k_ref, vt_ref, kc_ref, vct_ref, o_ref, sa_ref, sb_ref, acc_ref):
    lane = lax.broadcasted_iota(jnp.int32, (TQ, 128), 1)
    lo = lane < 64
    lp = lam_ref[...]
    lam = (jnp.exp(jnp.sum(lp[0:1] * lp[1:2], axis=1, keepdims=True))
           - jnp.exp(jnp.sum(lp[2:3] * lp[3:4], axis=1, keepdims=True)) + LAM_INIT0)
    nblk = S // TK

    def key_block(j):
        if j == 0:
            return kc_ref[...], vct_ref[...], 0, CTX
        return (k_ref[(j - 1) * TK:j * TK, :], vt_ref[:, (j - 1) * TK:j * TK], CTX + (j - 1) * TK, TK)

    def load_q(t):
        q = q_ref[pl.ds(pl.multiple_of(t * TQ, TQ), TQ), :]
        zero = jnp.zeros_like(q)
        return jnp.where(lo, q, zero), jnp.where(lo, zero, q)

    def scores_block(qs, j, s_ref, m8):
        kblk, _, off, n = key_block(j)
        out = []
        for mi in range(2):
            s = _dot_nt(kblk, qs[mi])
            s_ref[mi, off:off + n, :] = s
            part = jnp.max(s.reshape(n // 8, 8, TQ), axis=0)
            out.append(part if m8 is None else jnp.maximum(m8[mi], part))
        return out

    def probs_block(j, s_ref, m):
        _, vtblk, off, n = key_block(j)
        aug = jnp.concatenate([vtblk, jnp.ones((ONES_ROWS, n), BF16)], axis=0)
        for mi in range(2):
            p = jnp.exp2(s_ref[mi, off:off + n, :] - m[mi]).astype(BF16)
            pv = _dot(aug, p)
            if j == 0:
                acc_ref[mi] = pv
            else:
                acc_ref[mi] += pv

    def finish(t):
        o = (acc_ref[0, 0:128, :] / acc_ref[0, 128:129, :]
             - lam * (acc_ref[1, 0:128, :] / acc_ref[1, 128:129, :]))
        o = o * lax.rsqrt(jnp.mean(o * o, axis=0, keepdims=True) + EPS)
        o = (o * g_ref[...]) * (1.0 - LAM_INIT0)
        o_ref[pl.ds(pl.multiple_of(t * TQ, TQ), TQ), :] = o.T.astype(o_ref.dtype)

    def col_max(m8):
        return [jnp.max(m8[mi], axis=0, keepdims=True) for mi in range(2)]

    def stage(t_score, s_write, t_prob, s_read, m_read):
        qs = load_q(t_score)
        m8 = None
        for j in range(nblk + 1):
            m8 = scores_block(qs, j, s_write, m8)
            probs_block(j, s_read, m_read)
        finish(t_prob)
        return col_max(m8)

    qs0 = load_q(0)
    m8 = None
    for j in range(nblk + 1):
        m8 = scores_block(qs0, j, sa_ref, m8)
    m_a0 = col_max(m8)

    ntile = S // TQ

    def body(i, m_a):
        t0 = 2 * i
        m_b = stage(t0 + 1, sb_ref, t0, sa_ref, m_a)
        return tuple(stage(jnp.minimum(t0 + 2, ntile - 1), sa_ref, t0 + 1, sb_ref, m_b))

    lax.fori_loop(0, ntile // 2, body, tuple(m_a0))


def _attn_call(lam_p, subln_g_col, q, k, vt, kc, vct):
    qspec = pl.BlockSpec((None, S, 128), lambda b, h: (b, 0, h))
    return pl.pallas_call(
        _attn_kernel,
        grid=(B, HEADS),
        in_specs=[pl.BlockSpec((4, 64), lambda b, h: (0, 0)), pl.BlockSpec((128, 1), lambda b, h: (0, 0)),
                  qspec, qspec,
                  pl.BlockSpec((None, 128, S), lambda b, h: (b, h, 0)),
                  pl.BlockSpec((None, CTX, 128), lambda b, h: (b, 0, h)),
                  pl.BlockSpec((None, 128, CTX), lambda b, h: (b, h, 0))],
        out_specs=qspec,
        out_shape=jax.ShapeDtypeStruct((B, S, DD), BF16),
        scratch_shapes=[pltpu.VMEM((2, CTX + S, TQ), F32), pltpu.VMEM((2, CTX + S, TQ), F32),
                        pltpu.VMEM((2, 128 + ONES_ROWS, TQ), F32)],
        compiler_params=_params("parallel", "parallel"),
        name="diffattn",
    )(lam_p, subln_g_col, q, k, vt, kc, vct)


def _cplx_block(m):
    return np.block([[m.real, -m.imag], [m.imag, m.real]])


@functools.lru_cache(maxsize=None)
def _dft_tables():
    k = np.arange(R)
    w_r = np.exp(-2j * np.pi * np.outer(k, k) / R)
    f_a = _cplx_block(w_r[:, :HALF])
    f_a_real = np.concatenate([w_r.real, w_r.imag], axis=0)
    g_a = _cplx_block(np.conj(w_r)[:HALF, :])
    tw = np.exp(-2j * np.pi * np.outer(k, k) / NFFT)
    f32 = lambda a: np.ascontiguousarray(a, dtype=np.float32)
    return (f32(f_a), f32(f_a_real), f32(g_a), f32(w_r.real), f32(w_r.imag), f32(tw.real), f32(tw.imag))


@functools.lru_cache(maxsize=None)
def _filter_positions():
    idx = np.arange(NFFT)
    pos = np.where(idx < S, idx, NFFT - idx) % S
    bands = (FILTER_EMB - 1) // 2
    t = np.linspace(0.0, 1.0, S)[:, None]
    ang = (2.0 * math.pi / S) * np.arange(S)[:, None] * np.linspace(1e-4, bands - 1, bands)[None, :]
    z = np.concatenate([t, np.cos(ang), -np.sin(ang)], axis=-1)
    z = np.pad(z, ((0, 0), (0, FILTER_EMB_PAD - FILTER_EMB)))
    return np.ascontiguousarray(z[pos], dtype=np.float32)


def _decay_rates():
    return jnp.abs(jnp.linspace(math.log(1e-2) / 1.5, math.log(1e-2) / 0.3, DH, dtype=F32))[None, :]


def _rope_tables():
    rows = S // GRID_W
    row = jnp.repeat(jnp.arange(rows, dtype=jnp.int32), GRID_W).astype(F32)
    col = jnp.tile(jnp.arange(GRID_W, dtype=jnp.int32), rows).astype(F32)
    m = 16
    inv = 10000.0 ** (-jnp.arange(m, dtype=F32) / m)
    ang_r = row[:, None] * inv[None, :]
    ang_c = col[:, None] * inv[None, :]
    cos = jnp.concatenate([jnp.cos(ang_r)] * 2 + [jnp.cos(ang_c)] * 2, axis=1)
    sin = jnp.concatenate([-jnp.sin(ang_r), jnp.sin(ang_r), -jnp.sin(ang_c), jnp.sin(ang_c)], axis=1)
    return jnp.concatenate([cos, cos], axis=1), jnp.concatenate([sin, sin], axis=1)


TR_F = 512


def _filter_kernel(z_ref, w1_ref, b1_ref, fr_ref, w2_ref, b2_ref, w3_ref, dec_ref, o_ref):
    i = pl.program_id(0)
    z = z_ref[...]
    fr = fr_ref[...]
    h = jnp.sin(fr[0:1] * (_dot(z, w1_ref[...], precision=HI) + b1_ref[...]))
    h = jnp.sin(fr[1:2] * (_dot(h, w2_ref[...], precision=HI) + b2_ref[...]))
    window = jnp.exp(-z[:, 0:1] * dec_ref[...]) + 0.05
    ridx = i * TR_F + lax.broadcasted_iota(jnp.int32, (TR_F, 1), 0)
    for o in range(2):
        o_ref[o] = jnp.where(ridx == S, 0.0, _dot(h, w3_ref[o], precision=HI) * window)


def _filter_call(zfull, w1p, b1, freq, w2, b2, w3r, dec):
    full = lambda shape: pl.BlockSpec(shape, lambda i: (0,) * len(shape))
    half_tiles = S // TR_F
    return pl.pallas_call(
        _filter_kernel,
        grid=(NFFT // TR_F,),
        in_specs=[pl.BlockSpec((TR_F, FILTER_EMB_PAD), lambda i: (i, 0)),
                  full((FILTER_EMB_PAD, FILTER_WIDTH)), full((1, FILTER_WIDTH)), full((2, FILTER_WIDTH)),
                  full((FILTER_WIDTH, FILTER_WIDTH)), full((1, FILTER_WIDTH)),
                  pl.BlockSpec((2, None, FILTER_WIDTH, DH), lambda i: (0, i // half_tiles, 0, 0)),
                  full((1, DH))],
        out_specs=pl.BlockSpec((2, TR_F, DH), lambda i: (0, i, 0)),
        out_shape=jax.ShapeDtypeStruct((2, NFFT, DH), F32),
        compiler_params=_params("parallel"),
        name="hyena_filter",
    )(zfull, w1p, b1, freq, w2, b2, w3r, dec)


NB = 8


def _split_bf16(a):
    hi = a.astype(BF16)
    return hi, (a - hi.astype(F32)).astype(BF16)


def _dot3(a, b):
    a_hi, a_lo = _split_bf16(a)
    b_hi, b_lo = _split_bf16(b)
    return _dot(a_hi, b_hi) + (_dot(a_hi, b_lo) + _dot(a_lo, b_hi))


def _fspec_a_kernel(f_ref, h_ref, o_ref, x_ref):
    f_hi, f_lo = _split_bf16(f_ref[...])
    for j in range(NB):
        x_ref[j % 2] = h_ref[:, j, :]
        x_hi, x_lo = _split_bf16(x_ref[j % 2])
        o_ref[j] = _dot(f_hi, x_hi) + (_dot(f_hi, x_lo) + _dot(f_lo, x_hi))


def _fspec_a_call(f_a_real, hview):
    return pl.pallas_call(
        _fspec_a_kernel,
        grid=(2, R // NB),
        in_specs=[pl.BlockSpec((2 * R, R), lambda o, j: (0, 0)),
                  pl.BlockSpec((None, R, NB, DH), lambda o, j: (o, 0, j, 0))],
        out_specs=pl.BlockSpec((None, NB, 2 * R, DH), lambda o, j: (o, j, 0, 0)),
        out_shape=jax.ShapeDtypeStruct((2, R, 2 * R, DH), F32),
        scratch_shapes=[pltpu.VMEM((2, R, DH), F32)],
        compiler_params=_params("parallel", "parallel"),
        name="hyena_fspec_a",
    )(f_a_real, hview)


def _rows2(ref, j):
    return jnp.concatenate([ref[:, 0, j, :], ref[:, 1, j, :]], axis=0)


def _conv_a_kernel(f_ref, z_ref, o_ref):
    f = f_ref[...]
    for j in range(NB):
        zz = jnp.concatenate([z_ref[0, :, j, :], z_ref[1, :, j, :]], axis=0).astype(BF16)
        o_ref[j] = _dot(f, zz)


def _conv_a_call(f_a, zview):
    return pl.pallas_call(
        _conv_a_kernel,
        grid=(2, R // NB),
        in_specs=[pl.BlockSpec((2 * R, R), lambda p, j: (0, 0)),
                  pl.BlockSpec((2, None, HALF, NB, DH), lambda p, j: (0, p, 0, j, 0))],
        out_specs=pl.BlockSpec((None, NB, 2 * R, DH), lambda p, j: (p, j, 0, 0)),
        out_shape=jax.ShapeDtypeStruct((2, R, 2 * R, DH), F32),
        compiler_params=_params("parallel", "parallel"),
        name="hyena_conv_a",
    )(f_a, zview)


def _conv_c_kernel(wr_ref, wi_ref, tr_ref, ti_ref, af_ref, a_ref, o_ref, x_ref):
    wr, wi = wr_ref[...], wi_ref[...]
    for kk in range(NB):
        tr, ti = tr_ref[kk:kk + 1, :], ti_ref[kk:kk + 1, :]
        mr = wr * tr - wi * ti
        mi = wr * ti + wi * tr
        mb = jnp.concatenate([jnp.concatenate([mr, -mi], axis=1), jnp.concatenate([mi, mr], axis=1)], axis=0)
        m_hi, m_lo = _split_bf16(mb)
        x_ref[kk % 2] = _rows2(af_ref, kk)
        f_hi, f_lo = _split_bf16(x_ref[kk % 2])
        h = (_dot(m_hi, f_hi) + (_dot(m_hi, f_lo) + _dot(m_lo, f_hi))) * (1.0 / NFFT)
        hr, hi = h[0:R], h[R:2 * R]
        a = jnp.concatenate([_rows2(a_ref.at[0], kk), _rows2(a_ref.at[1], kk)], axis=1).astype(BF16)
        x = _dot(m_hi, a)
        ys = []
        for p in range(2):
            xr, xi = x[0:R, p * DH:(p + 1) * DH], x[R:2 * R, p * DH:(p + 1) * DH]
            ys.append(jnp.concatenate([xr * hr - xi * hi, xr * hi + xi * hr], axis=0))
        b = _dot(mb.T.astype(BF16), jnp.concatenate(ys, axis=1).astype(BF16))
        o_ref[0, kk] = b[:, 0:DH]
        o_ref[1, kk] = b[:, DH:2 * DH]


def _conv_c_call(w_re, w_im, tw_re, tw_im, af_view, order, a_view):
    full = pl.BlockSpec((R, R), lambda k: (0, 0))
    trow = pl.BlockSpec((NB, R), lambda k: (k, 0))
    return pl.pallas_call(
        _conv_c_kernel,
        grid=(R // NB,),
        in_specs=[full, full, trow, trow,
                  pl.BlockSpec((None, R, 2, NB, DH), lambda k: (order, 0, 0, k, 0)),
                  pl.BlockSpec((2, R, 2, NB, DH), lambda k: (0, 0, 0, k, 0))],
        out_specs=pl.BlockSpec((2, NB, 2 * R, DH), lambda k: (0, k, 0, 0)),
        out_shape=jax.ShapeDtypeStruct((2, R, 2 * R, DH), F32),
        scratch_shapes=[pltpu.VMEM((2, 2 * R, DH), F32)],
        compiler_params=_params("parallel"),
        name="hyena_conv_c",
    )(w_re, w_im, tw_re, tw_im, af_view, a_view)


def _conv_out_kernel(first_order, g_ref, f_ref, b_ref, z_ref, gate_ref, d_ref, zo_ref, *next_a):
    g, d = g_ref[...], d_ref[...]
    for j in range(NB):
        y = _dot(g, _rows2(b_ref, j).astype(BF16))
        zn = []
        for bi in range(2):
            zc = z_ref[bi, :, j, :] if first_order else z_ref[bi, j]
            zn.append(gate_ref[bi, :, j, :] * (y[bi * HALF:(bi + 1) * HALF] + d * zc))
            zo_ref[bi, j] = zn[bi]
        if first_order:
            next_a[0][j] = _dot(f_ref[...], jnp.concatenate(zn, axis=0).astype(BF16))


def _conv_out_call(g_a, f_a, b_view, z_in, gate_view, drow, first_order):
    pm = pl.BlockSpec((2, None, HALF, NB, DH), lambda p, j: (0, p, 0, j, 0))
    nm = pl.BlockSpec((2, None, NB, HALF, DH), lambda p, j: (0, p, j, 0, 0))
    out_specs = [nm]
    out_shape = [jax.ShapeDtypeStruct((2, 2, R, HALF, DH), F32)]
    if first_order:
        out_specs.append(pl.BlockSpec((None, NB, 2 * R, DH), lambda p, j: (p, j, 0, 0)))
        out_shape.append(jax.ShapeDtypeStruct((2, R, 2 * R, DH), F32))
    return pl.pallas_call(
        functools.partial(_conv_out_kernel, first_order),
        grid=(2, R // NB),
        in_specs=[pl.BlockSpec((R, 2 * R), lambda p, j: (0, 0)), pl.BlockSpec((2 * R, R), lambda p, j: (0, 0)),
                  pl.BlockSpec((None, R, 2, NB, DH), lambda p, j: (p, 0, 0, j, 0)),
                  pm if first_order else nm, pm, pl.BlockSpec((1, DH), lambda p, j: (0, 0))],
        out_specs=out_specs,
        out_shape=out_shape,
        compiler_params=_params("parallel", "parallel"),
        name="hyena_conv_out0" if first_order else "hyena_conv_out1",
    )(g_a, f_a, b_view, z_in, gate_view, drow)


TM_OUT = NB * R


def _outproj_kernel(x_ref, ya_ref, yb_ref, wa_ref, wb_ref, g1_ref, o_ref):
    ya = jnp.concatenate([ya_ref[:, r, :] for r in range(NB)], axis=0).astype(BF16)
    mix = _dot(ya, wa_ref[...]) + _dot(yb_ref[...], wb_ref[...])
    o_ref[...] = x_ref[...] + g1_ref[...] * mix


def _outproj_call(x2, ya_nm, yb, wa, wb, g1):
    tm = TM_OUT
    tiles = S // tm
    row = lambda i: (i, 0)
    return pl.pallas_call(
        _outproj_kernel,
        grid=(T // tm,),
        in_specs=[pl.BlockSpec((tm, D), row),
                  pl.BlockSpec((None, R, NB, DH), lambda i: (i // tiles, 0, i % tiles, 0)),
                  pl.BlockSpec((tm, DD), row),
                  pl.BlockSpec((DH, D), lambda i: (0, 0)), pl.BlockSpec((DD, D), lambda i: (0, 0)),
                  pl.BlockSpec((None, 1, D), lambda i: (i // tiles, 0, 0))],
        out_specs=pl.BlockSpec((tm, D), row),
        out_shape=jax.ShapeDtypeStruct((T, D), F32),
        compiler_params=_params("parallel"),
        name="outproj",
    )(x2, ya_nm, yb, wa, wb, g1)


TM_FFN = 512
FH = 16
CK = 256
assert D_FF % CK == 0


def _ffn_kernel(final, xp_ref, xm_ref, xn_ref, g_ref, sh_ref, sc_ref, g2_ref, wup_ref, cw_ref, cb_ref, wdn_ref,
                fg_ref, o_ref, hn_ref, ug_ref, uu_ref, act_ref):
    i = pl.program_id(0)
    tiles = S // TM_FFN
    first = (i % tiles) == 0
    last = (i % tiles) == tiles - 1
    g, sh, sc = g_ref[...], sh_ref[...], sc_ref[...]
    xm = xm_ref[...]
    hn_ref[0:FH, :] = jnp.where(first, 0.0, _rms_mod(xp_ref[...], g, sh, sc)).astype(BF16)
    hn_ref[FH:FH + TM_FFN, :] = _rms_mod(xm, g, sh, sc).astype(BF16)
    hn_ref[FH + TM_FFN:2 * FH + TM_FFN, :] = jnp.where(last, 0.0, _rms_mod(xn_ref[...], g, sh, sc)).astype(BF16)
    hn = hn_ref[...]

    def conv(u_ref, c0):
        w = cw_ref[:, c0:c0 + CK]
        return (w[0:1] * u_ref[FH - 1:FH - 1 + TM_FFN, :] + w[1:2] * u_ref[FH:FH + TM_FFN, :]
                + w[2:3] * u_ref[FH + 1:FH + 1 + TM_FFN, :] + cb_ref[:, c0:c0 + CK])

    for c in range(D_FF // CK):
        c0 = c * CK
        ug, uu = ug_ref.at[c % 2], uu_ref.at[c % 2]
        ug[...] = _dot(hn, wup_ref[:, c0:c0 + CK])
        uu[...] = _dot(hn, wup_ref[:, D_FF + c0:D_FF + c0 + CK])
        act_ref[:, c0:c0 + CK] = (_silu(conv(ug, c0)) * conv(uu, D_FF + c0)).astype(BF16)
    out = xm + g2_ref[...] * _dot(act_ref[...], wdn_ref[...])
    if final:
        out = (out * lax.rsqrt(jnp.mean(out * out, axis=-1, keepdims=True) + EPS)) * fg_ref[...]
    o_ref[...] = out


def _ffn_call(x2, g, sh, sc, g2, wup, cw, cb, wdn, fg, final):
    tm = TM_FFN
    tiles = S // tm
    nh = tm // FH
    row = lambda i: (i, 0)
    bvec = pl.BlockSpec((None, 1, D), lambda i: (i // tiles, 0, 0))
    full = lambda shape: pl.BlockSpec(shape, lambda i: (0,) * len(shape))
    return pl.pallas_call(
        functools.partial(_ffn_kernel, final),
        grid=(T // tm,),
        in_specs=[
            pl.BlockSpec((FH, D), lambda i: (jnp.maximum(i * nh - 1, 0), 0)),
            pl.BlockSpec((tm, D), row),
            pl.BlockSpec((FH, D), lambda i: (jnp.minimum((i + 1) * nh, T // FH - 1), 0)),
            full((1, D)), bvec, bvec, bvec,
            full((D, 2 * D_FF)), full((3, 2 * D_FF)), full((1, 2 * D_FF)), full((D_FF, D)), full((1, D)),
        ],
        out_specs=pl.BlockSpec((tm, D), row),
        out_shape=jax.ShapeDtypeStruct((T, D), F32),
        scratch_shapes=[pltpu.VMEM((tm + 2 * FH, D), BF16), pltpu.VMEM((2, tm + 2 * FH, CK), F32),
                        pltpu.VMEM((2, tm + 2 * FH, CK), F32), pltpu.VMEM((tm, D_FF), BF16)],
        compiler_params=_params("parallel"),
        name="convffn_final" if final else "convffn",
    )(x2, x2, x2, g, sh, sc, g2, wup, cw, cb, wdn, fg)


TM_SGU = 512


def _sgu_kernel(x_ref, g_ref, sh_ref, sc_ref, g1_ref, win_ref, bin_ref, lng_ref, lnb_ref, ws_ref, bs_ref, wout_ref,
                o_ref, s_ref):
    x = x_ref[...]
    hn = _rms_mod(x, g_ref[...], sh_ref[...], sc_ref[...]).astype(BF16)
    pre = _dot(hn, win_ref[...]) + bin_ref[...]
    act = 0.5 * pre * (1.0 + lax.erf(pre * (2.0 ** -0.5)))
    u = act[:, 0:D]
    v = act[:, D:2 * D]
    mu = jnp.mean(v, axis=-1, keepdims=True)
    vc = v - mu
    v = (vc * lax.rsqrt(jnp.mean(vc * vc, axis=-1, keepdims=True) + EPS)) * lng_ref[...] + lnb_ref[...]
    vb = v.astype(BF16)
    cg = D // GROUPS
    for n in range(TM_SGU // CHUNK):
        for gi in range(GROUPS):
            s_ref[n * CHUNK:(n + 1) * CHUNK, gi * cg:(gi + 1) * cg] = (
                _dot(ws_ref[gi], vb[n * CHUNK:(n + 1) * CHUNK, gi * cg:(gi + 1) * cg])
                + bs_ref[:, gi * cg:(gi + 1) * cg])
    gated = (u * s_ref[...]).astype(BF16)
    o_ref[...] = x + g1_ref[...] * _dot(gated, wout_ref[...])


def _sgu_call(x2, g, sh, sc, g1, win, bin_, lng, lnb, ws, bs_full, wout):
    tm = TM_SGU
    tiles = S // tm
    row = lambda i: (i, 0)
    bvec = pl.BlockSpec((None, 1, D), lambda i: (i // tiles, 0, 0))
    full = lambda shape: pl.BlockSpec(shape, lambda i: (0,) * len(shape))
    return pl.pallas_call(
        _sgu_kernel,
        grid=(T // tm,),
        in_specs=[pl.BlockSpec((tm, D), row), full((1, D)), bvec, bvec, bvec,
                  full((D, 2 * D)), full((1, 2 * D)), full((1, D)), full((1, D)),
                  full((GROUPS, CHUNK, CHUNK)), full((CHUNK, D)), full((D, D))],
        out_specs=pl.BlockSpec((tm, D), row),
        out_shape=jax.ShapeDtypeStruct((T, D), F32),
        scratch_shapes=[pltpu.VMEM((tm, D), F32)],
        compiler_params=_params("parallel"),
        name="sgu",
    )(x2, g, sh, sc, g1, win, bin_, lng, lnb, ws, bs_full, wout)


def kernel(x, c, ctx, c_ctx, mod_w, mod_b, norm_mix_g, norm_ffn_g, ffn_w_up, ffn_conv_w, ffn_conv_b, ffn_w_down,
           ab_w_in, hy_conv_w, hy_conv_b, hy_w1, hy_b1, hy_freq, hy_w2, hy_b2, hy_w3, hy_bias, da_lambda,
           da_subln_g, ab_w_out, sgu_w_in, sgu_b_in, sgu_ln_g, sgu_ln_b, sgu_w_s, sgu_b_s, sgu_w_out,
           final_norm_g):
    x2 = x.reshape(T, D)
    cc = jnp.concatenate([c, c_ctx[None, :], jnp.zeros((3, D), F32)], axis=0)
    mods = _mod_call(cc, mod_w, mod_b)

    def mod_vec(layer, k):
        return mods[layer, 0:B, k * D:(k + 1) * D].reshape(B, 1, D)

    w_in = ab_w_in[0].astype(BF16)
    n_hq = 3 * DH + DD
    cos_t, sin_t = _rope_tables()
    w_k = w_in[:, n_hq:n_hq + DD]
    w_vt = w_in[:, n_hq + DD:].T
    hv, hx1, hx2, q, k, vt = _inproj_call(
        x2, norm_mix_g[0][None, :], mod_vec(0, 0), mod_vec(0, 1),
        w_in[:, 0:3 * DH], jnp.concatenate([w_in[:, 3 * DH:n_hq], w_k], axis=1), w_vt,
        cos_t, sin_t, hy_conv_w[0], hy_conv_b[0][None, :])
    kc, vct = _ctxkv_call(ctx.reshape(B * CTX, D), norm_mix_g[0][None, :], mods[0, 4:5, 0:D], mods[0, 4:5, D:2 * D],
                          w_k, w_vt)
    yb = _attn_call(da_lambda[0], da_subln_g[0][:, None], q.reshape(B, S, DD), k.reshape(B, S, DD), vt,
                    kc.reshape(B, CTX, DD), vct)

    f_a, f_a_real, g_a, w_re, w_im, tw_re, tw_im = (jnp.asarray(t) for t in _dft_tables())
    w1p = jnp.pad(hy_w1[0], ((0, FILTER_EMB_PAD - FILTER_EMB), (0, 0)))
    w3r = hy_w3[0].reshape(FILTER_WIDTH, 2, 2, DH).transpose(1, 2, 0, 3)
    filt = _filter_call(jnp.asarray(_filter_positions()), w1p, hy_b1[0][None, :], hy_freq[0], hy_w2[0],
                        hy_b2[0][None, :], w3r, _decay_rates())
    af_view = _fspec_a_call(f_a_real, filt.reshape(2, R, R, DH)).reshape(2, R, 2, R, DH)

    f_a_b, g_a_b = f_a.astype(BF16), g_a.astype(BF16)
    pm = lambda t: t.reshape(2, 2, HALF, R, DH)
    a = _conv_a_call(f_a_b, pm(hv))
    bq = _conv_c_call(w_re, w_im, tw_re, tw_im, af_view, 0, a.reshape(2, R, 2, R, DH))
    z1, a = _conv_out_call(g_a_b, f_a_b, bq.reshape(2, R, 2, R, DH), pm(hv), pm(hx1), hy_bias[0, 0][None, :], True)
    bq = _conv_c_call(w_re, w_im, tw_re, tw_im, af_view, 1, a.reshape(2, R, 2, R, DH))
    (z2,) = _conv_out_call(g_a_b, f_a_b, bq.reshape(2, R, 2, R, DH), z1, pm(hx2), hy_bias[0, 1][None, :], False)

    w_out = ab_w_out[0].astype(BF16)
    x2 = _outproj_call(x2, z2.reshape(B, R, HALF, DH), yb.reshape(T, DD), w_out[0:DH], w_out[DH:], mod_vec(0, 2))
    x2 = _ffn_call(x2, norm_ffn_g[0][None, :], mod_vec(0, 3), mod_vec(0, 4), mod_vec(0, 5),
                   ffn_w_up[0].astype(BF16), ffn_conv_w[0], ffn_conv_b[0][None, :], ffn_w_down[0].astype(BF16),
                   final_norm_g[None, :], False)

    bs_full = jnp.repeat(sgu_b_s[0].T, D // GROUPS, axis=1)
    x2 = _sgu_call(x2, norm_mix_g[1][None, :], mod_vec(1, 0), mod_vec(1, 1), mod_vec(1, 2),
                   sgu_w_in[0].astype(BF16), sgu_b_in[0][None, :], sgu_ln_g[0][None, :], sgu_ln_b[0][None, :],
                   sgu_w_s[0].astype(BF16), bs_full, sgu_w_out[0].astype(BF16))
    x2 = _ffn_call(x2, norm_ffn_g[1][None, :], mod_vec(1, 3), mod_vec(1, 4), mod_vec(1, 5),
                   ffn_w_up[1].astype(BF16), ffn_conv_w[1], ffn_conv_b[1][None, :], ffn_w_down[1].astype(BF16),
                   final_norm_g[None, :], True)
    return x2.reshape(B, S, D)
```

```python
import functools
import math

import numpy as np
import jax
import jax.numpy as jnp
from jax import lax
from jax.experimental import pallas as pl
from jax.experimental.pallas import tpu as pltpu

D = 1024
B = 4
S = 8192
T = B * S
GRID_W = 64
CTX = 256
EPS = 1e-6
DH = 512
DD = 512
HEADS = 4
FILTER_EMB = 33
FILTER_EMB_PAD = 64
FILTER_WIDTH = 64
D_FF = 2816
CHUNK = 128
GROUPS = 8

NFFT = 2 * S
R = 128
HALF = R // 2
WCOL = R * DH

F32 = jnp.float32
BF16 = jnp.bfloat16
HI = lax.Precision.HIGHEST

VMEM_LIMIT = 56 * 1024 * 1024


def _params(*sem):
    return pltpu.CompilerParams(dimension_semantics=sem, vmem_limit_bytes=VMEM_LIMIT)


def _rms_mod(x, g, sh, sc):
    y = x * lax.rsqrt(jnp.mean(x * x, axis=-1, keepdims=True) + EPS)
    return (y * g) * (1.0 + sc) + sh


def _silu(x):
    return x * (1.0 / (1.0 + jnp.exp(-x)))


def _dot(a, b, **kw):
    return jnp.dot(a, b, preferred_element_type=F32, **kw)


def _mod_kernel(c_ref, w_ref, b_ref, o_ref):
    o_ref[...] = _dot(_silu(c_ref[...]), w_ref[...], precision=HI) + b_ref[...]


def _mod_call(cc, mod_w, mod_b):
    depth = mod_w.shape[0]
    tn = 1024
    return pl.pallas_call(
        _mod_kernel,
        grid=(depth, 6 * D // tn),
        in_specs=[
            pl.BlockSpec((8, D), lambda l, j: (0, 0)),
            pl.BlockSpec((None, D, tn), lambda l, j: (l, 0, j)),
            pl.BlockSpec((None, 1, tn), lambda l, j: (l, 0, j)),
        ],
        out_specs=pl.BlockSpec((None, 8, tn), lambda l, j: (l, 0, j)),
        out_shape=jax.ShapeDtypeStruct((depth, 8, 6 * D), F32),
        compiler_params=_params("parallel", "parallel"),
        name="mod",
    )(cc, mod_w, mod_b.reshape(depth, 1, 6 * D))


TM_IN = 512
HALO = 8


def _dot_nt(a, b):
    return lax.dot_general(a, b, (((1,), (1,)), ((), ())), preferred_element_type=F32)


def _inproj_kernel(xp_ref, xm_ref, xn_ref, g_ref, sh_ref, sc_ref, why_ref, wqk_ref, wvt_ref, cos_ref, sin_ref,
                   cw_ref, cb_ref, hv_ref, hx1_ref, hx2_ref, q_ref, k_ref, vt_ref, buf_ref):
    i = pl.program_id(0)
    tiles = S // TM_IN
    first = (i % tiles) == 0
    last = (i % tiles) == tiles - 1
    g, sh, sc = g_ref[...], sh_ref[...], sc_ref[...]
    hn = _rms_mod(xm_ref[...], g, sh, sc).astype(BF16)
    hp = jnp.where(first, 0.0, _rms_mod(xp_ref[...], g, sh, sc)).astype(BF16)
    hx = jnp.where(last, 0.0, _rms_mod(xn_ref[...], g, sh, sc)).astype(BF16)
    why = why_ref[...]
    buf_ref[0:HALO, :] = _dot(hp, why)
    buf_ref[HALO:HALO + TM_IN, :] = _dot(hn, why)
    buf_ref[HALO + TM_IN:2 * HALO + TM_IN, :] = _dot(hx, why)
    cw = cw_ref[...]
    conv = (cw[0:1] * buf_ref[HALO - 1:HALO - 1 + TM_IN, :] + cw[1:2] * buf_ref[HALO:HALO + TM_IN, :]
            + cw[2:3] * buf_ref[HALO + 1:HALO + 1 + TM_IN, :] + cb_ref[...])
    hv_ref[...] = conv[:, 0:DH]
    hx1_ref[...] = conv[:, DH:2 * DH]
    hx2_ref[...] = conv[:, 2 * DH:3 * DH]

    qkv = _dot(hn, wqk_ref[...])
    vt_ref[...] = _dot_nt(wvt_ref[...], hn).astype(BF16)
    cos, sin = cos_ref[...], sin_ref[...]
    lane = lax.broadcasted_iota(jnp.int32, (TM_IN, 128), 1)
    even = ((lane // 16) % 2) == 0

    def rope(xg):
        partner = jnp.where(even, pltpu.roll(xg, 112, 1), pltpu.roll(xg, 16, 1))
        return xg * cos + partner * sin

    scale = 64 ** -0.5 * math.log2(math.e)
    for h in range(HEADS):
        q_ref[:, h * 128:(h + 1) * 128] = (rope(qkv[:, h * 128:(h + 1) * 128]) * scale).astype(BF16)
        k_ref[:, h * 128:(h + 1) * 128] = rope(qkv[:, DD + h * 128:DD + (h + 1) * 128]).astype(BF16)


def _inproj_call(x2, g, sh, sc, why, wqk, wvt, cos_t, sin_t, cw, cb):
    tm = TM_IN
    tiles = S // tm
    nh = tm // HALO
    row = lambda i: (i, 0)
    bvec = pl.BlockSpec((None, 1, D), lambda i: (i // tiles, 0, 0))
    full = lambda shape: pl.BlockSpec(shape, lambda i: (0,) * len(shape))
    return pl.pallas_call(
        _inproj_kernel,
        grid=(T // tm,),
        in_specs=[
            pl.BlockSpec((HALO, D), lambda i: (jnp.maximum(i * nh - 1, 0), 0)),
            pl.BlockSpec((tm, D), row),
            pl.BlockSpec((HALO, D), lambda i: (jnp.minimum((i + 1) * nh, T // HALO - 1), 0)),
            full((1, D)), bvec, bvec,
            full((D, 3 * DH)), full((D, 2 * DD)), full((DD, D)),
            pl.BlockSpec((tm, 128), lambda i: (i % tiles, 0)),
            pl.BlockSpec((tm, 128), lambda i: (i % tiles, 0)),
            full((3, 3 * DH)), full((1, 3 * DH)),
        ],
        out_specs=[pl.BlockSpec((tm, DH), row)] * 3 + [pl.BlockSpec((tm, DD), row)] * 2
        + [pl.BlockSpec((None, DD, tm), lambda i: (i // tiles, 0, i % tiles))],
        out_shape=[jax.ShapeDtypeStruct((T, DH), F32)] * 3 + [jax.ShapeDtypeStruct((T, DD), BF16)] * 2
        + [jax.ShapeDtypeStruct((B, DD, S), BF16)],
        scratch_shapes=[pltpu.VMEM((tm + 2 * HALO, 3 * DH), F32)],
        compiler_params=_params("parallel"),
        name="inproj",
    )(x2, x2, x2, g, sh, sc, why, wqk, wvt, cos_t, sin_t, cw, cb)


def _ctxkv_kernel(x_ref, g_ref, sh_ref, sc_ref, wk_ref, wvt_ref, k_ref, vt_ref):
    hn = _rms_mod(x_ref[...], g_ref[...], sh_ref[...], sc_ref[...]).astype(BF16)
    k_ref[...] = _dot(hn, wk_ref[...]).astype(BF16)
    vt_ref[...] = _dot_nt(wvt_ref[...], hn).astype(BF16)


def _ctxkv_call(ctx2, g, sh, sc, wk, wvt):
    tm = CTX
    full = lambda shape: pl.BlockSpec(shape, lambda i: (0,) * len(shape))
    return pl.pallas_call(
        _ctxkv_kernel,
        grid=(B * CTX // tm,),
        in_specs=[pl.BlockSpec((tm, D), lambda i: (i, 0)), full((1, D)), full((1, D)), full((1, D)),
                  full((D, DD)), full((DD, D))],
        out_specs=[pl.BlockSpec((tm, DD), lambda i: (i, 0)), pl.BlockSpec((None, DD, tm), lambda i: (i, 0, 0))],
        out_shape=[jax.ShapeDtypeStruct((B * CTX, DD), BF16), jax.ShapeDtypeStruct((B, DD, CTX), BF16)],
        compiler_params=_params("parallel"),
        name="ctxkv",
    )(ctx2, g, sh, sc, wk, wvt)


TQ = 256
TK = 1024
ONES_ROWS = 16
LAM_INIT0 = 0.8 - 0.6 * math.exp(-0.3 * 0)


def _attn_kernel(lam_ref, g_ref, q_ref, k_ref, vt_ref, kc_ref, vct_ref, o_ref, sa_ref, sb_ref, acc_ref):
    lane = lax.broadcasted_iota(jnp.int32, (TQ, 128), 1)
    lo = lane < 64
    lp = lam_ref[...]
    lam = (jnp.exp(jnp.sum(lp[0:1] * lp[1:2], axis=1, keepdims=True))
           - jnp.exp(jnp.sum(lp[2:3] * lp[3:4], axis=1, keepdims=True)) + LAM_INIT0)
    nblk = S // TK

    def key_block(j):
        if j == 0:
            return kc_ref[...], vct_ref[...], 0, CTX
        return (k_ref[(j - 1) * TK:j * TK, :], vt_ref[:, (j - 1) * TK:j * TK], CTX + (j - 1) * TK, TK)

    def load_q(t):
        q = q_ref[pl.ds(pl.multiple_of(t * TQ, TQ), TQ), :]
        zero = jnp.zeros_like(q)
        return jnp.where(lo, q, zero), jnp.where(lo, zero, q)

    def scores_block(qs, j, s_ref, m8):
        kblk, _, off, n = key_block(j)
        out = []
        for mi in range(2):
            s = _dot_nt(kblk, qs[mi])
            s_ref[mi, off:off + n, :] = s
            part = jnp.max(s.reshape(n // 8, 8, TQ), axis=0)
            out.append(part if m8 is None else jnp.maximum(m8[mi], part))
        return out

    def probs_block(j, s_ref, m):
        _, vtblk, off, n = key_block(j)
        aug = jnp.concatenate([vtblk, jnp.ones((ONES_ROWS, n), BF16)], axis=0)
        for mi in range(2):
            p = jnp.exp2(s_ref[mi, off:off + n, :] - m[mi]).astype(BF16)
            pv = _dot(aug, p)
            if j == 0:
                acc_ref[mi] = pv
            else:
                acc_ref[mi] += pv

    def finish(t):
        o = (acc_ref[0, 0:128, :] / acc_ref[0, 128:129, :]
             - lam * (acc_ref[1, 0:128, :] / acc_ref[1, 128:129, :]))
        o = o * lax.rsqrt(jnp.mean(o * o, axis=0, keepdims=True) + EPS)
        o = (o * g_ref[...]) * (1.0 - LAM_INIT0)
        o_ref[pl.ds(pl.multiple_of(t * TQ, TQ), TQ), :] = o.T.astype(o_ref.dtype)

    def col_max(m8):
        return [jnp.max(m8[mi], axis=0, keepdims=True) for mi in range(2)]

    def stage(t_score, s_write, t_prob, s_read, m_read):
        qs = load_q(t_score)
        m8 = None
        for j in range(nblk + 1):
            m8 = scores_block(qs, j, s_write, m8)
            probs_block(j, s_read, m_read)
        finish(t_prob)
        return col_max(m8)

    qs0 = load_q(0)
    m8 = None
    for j in range(nblk + 1):
        m8 = scores_block(qs0, j, sa_ref, m8)
    m_a0 = col_max(m8)

    ntile = S // TQ

    def body(t, m):
        t_next = jnp.minimum(t + 1, ntile - 1)
        return lax.cond(t % 2 == 0,
                        lambda m: tuple(stage(t_next, sb_ref, t, sa_ref, m)),
                        lambda m: tuple(stage(t_next, sa_ref, t, sb_ref, m)), m)

    lax.fori_loop(0, ntile, body, tuple(m_a0))


def _attn_call(lam_p, subln_g_col, q, k, vt, kc, vct):
    qspec = pl.BlockSpec((None, S, 128), lambda b, h: (b, 0, h))
    return pl.pallas_call(
        _attn_kernel,
        grid=(B, HEADS),
        in_specs=[pl.BlockSpec((4, 64), lambda b, h: (0, 0)), pl.BlockSpec((128, 1), lambda b, h: (0, 0)),
                  qspec, qspec,
                  pl.BlockSpec((None, 128, S), lambda b, h: (b, h, 0)),
                  pl.BlockSpec((None, CTX, 128), lambda b, h: (b, 0, h)),
                  pl.BlockSpec((None, 128, CTX), lambda b, h: (b, h, 0))],
        out_specs=qspec,
        out_shape=jax.ShapeDtypeStruct((B, S, DD), BF16),
        scratch_shapes=[pltpu.VMEM((2, CTX + S, TQ), F32), pltpu.VMEM((2, CTX + S, TQ), F32),
                        pltpu.VMEM((2, 128 + ONES_ROWS, TQ), F32)],
        compiler_params=_params("parallel", "parallel"),
        name="diffattn",
    )(lam_p, subln_g_col, q, k, vt, kc, vct)


def _cplx_block(m):
    return np.block([[m.real, -m.imag], [m.imag, m.real]])


@functools.lru_cache(maxsize=None)
def _dft_tables():
    k = np.arange(R)
    w_r = np.exp(-2j * np.pi * np.outer(k, k) / R)
    f_a = _cplx_block(w_r[:, :HALF])
    f_a_real = np.concatenate([w_r.real, w_r.imag], axis=0)
    g_a = _cplx_block(np.conj(w_r)[:HALF, :])
    tw = np.exp(-2j * np.pi * np.outer(k, k) / NFFT)
    f32 = lambda a: np.ascontiguousarray(a, dtype=np.float32)
    return (f32(f_a), f32(f_a_real), f32(g_a), f32(w_r.real), f32(w_r.imag), f32(tw.real), f32(tw.imag))


@functools.lru_cache(maxsize=None)
def _filter_positions():
    idx = np.arange(NFFT)
    pos = np.where(idx < S, idx, NFFT - idx) % S
    bands = (FILTER_EMB - 1) // 2
    t = np.linspace(0.0, 1.0, S)[:, None]
    ang = (2.0 * math.pi / S) * np.arange(S)[:, None] * np.linspace(1e-4, bands - 1, bands)[None, :]
    z = np.concatenate([t, np.cos(ang), -np.sin(ang)], axis=-1)
    z = np.pad(z, ((0, 0), (0, FILTER_EMB_PAD - FILTER_EMB)))
    return np.ascontiguousarray(z[pos], dtype=np.float32)


def _decay_rates():
    return jnp.abs(jnp.linspace(math.log(1e-2) / 1.5, math.log(1e-2) / 0.3, DH, dtype=F32))[None, :]


@functools.lru_cache(maxsize=None)
def _rope_tables():
    rows = S // GRID_W
    row = np.repeat(np.arange(rows), GRID_W).astype(np.float64)
    col = np.tile(np.arange(GRID_W), rows).astype(np.float64)
    m = 16
    inv = 10000.0 ** (-np.arange(m) / m)
    ang_r = row[:, None] * inv[None, :]
    ang_c = col[:, None] * inv[None, :]
    cos = np.concatenate([np.cos(ang_r)] * 2 + [np.cos(ang_c)] * 2, axis=1)
    sin = np.concatenate([-np.sin(ang_r), np.sin(ang_r), -np.sin(ang_c), np.sin(ang_c)], axis=1)
    f32 = lambda a: np.ascontiguousarray(np.concatenate([a, a], axis=1), dtype=np.float32)
    return f32(cos), f32(sin)


TR_F = 512


def _filter_kernel(zt_ref, t_ref, w1t_ref, b1_ref, fr_ref, w2t_ref, b2_ref, w3_ref, dec_ref, o_ref):
    i = pl.program_id(0)
    fr = fr_ref[...]
    h = jnp.sin(fr[:, 0:1] * (_dot3(w1t_ref[...], zt_ref[...]) + b1_ref[...]))
    h = jnp.sin(fr[:, 1:2] * (_dot3(w2t_ref[...], h) + b2_ref[...]))
    h = h.T
    window = jnp.exp(-t_ref[...] * dec_ref[...]) + 0.05
    ridx = i * TR_F + lax.broadcasted_iota(jnp.int32, (TR_F, 1), 0)
    for o in range(2):
        o_ref[o] = jnp.where(ridx == S, 0.0, _dot3(h, w3_ref[o]) * window)


def _filter_call(zt, tcol, w1t, b1c, freqc, w2t, b2c, w3r, dec):
    full = lambda shape: pl.BlockSpec(shape, lambda i: (0,) * len(shape))
    half_tiles = S // TR_F
    return pl.pallas_call(
        _filter_kernel,
        grid=(NFFT // TR_F,),
        in_specs=[pl.BlockSpec((FILTER_EMB_PAD, TR_F), lambda i: (0, i)), pl.BlockSpec((TR_F, 1), lambda i: (i, 0)),
                  full((FILTER_WIDTH, FILTER_EMB_PAD)), full((FILTER_WIDTH, 1)), full((FILTER_WIDTH, 2)),
                  full((FILTER_WIDTH, FILTER_WIDTH)), full((FILTER_WIDTH, 1)),
                  pl.BlockSpec((2, None, FILTER_WIDTH, DH), lambda i: (0, i // half_tiles, 0, 0)),
                  full((1, DH))],
        out_specs=pl.BlockSpec((2, TR_F, DH), lambda i: (0, i, 0)),
        out_shape=jax.ShapeDtypeStruct((2, NFFT, DH), F32),
        compiler_params=_params("parallel"),
        name="hyena_filter",
    )(zt, tcol, w1t, b1c, freqc, w2t, b2c, w3r, dec)


NB = 8


def _split_bf16(a):
    hi = a.astype(BF16)
    return hi, (a - hi.astype(F32)).astype(BF16)


def _dot3(a, b):
    a_hi, a_lo = _split_bf16(a)
    b_hi, b_lo = _split_bf16(b)
    return _dot(a_hi, b_hi) + (_dot(a_hi, b_lo) + _dot(a_lo, b_hi))


def _fspec_a_kernel(f_ref, h_ref, o_ref, x_ref):
    f_hi, f_lo = _split_bf16(f_ref[...])
    for j in range(NB):
        x_ref[j % 2] = h_ref[:, j, :]
        x_hi, x_lo = _split_bf16(x_ref[j % 2])
        o_ref[j] = _dot(f_hi, x_hi) + (_dot(f_hi, x_lo) + _dot(f_lo, x_hi))


def _fspec_a_call(f_a_real, hview):
    return pl.pallas_call(
        _fspec_a_kernel,
        grid=(2, R // NB),
        in_specs=[pl.BlockSpec((2 * R, R), lambda o, j: (0, 0)),
                  pl.BlockSpec((None, R, NB, DH), lambda o, j: (o, 0, j, 0))],
        out_specs=pl.BlockSpec((None, NB, 2 * R, DH), lambda o, j: (o, j, 0, 0)),
        out_shape=jax.ShapeDtypeStruct((2, R, 2 * R, DH), F32),
        scratch_shapes=[pltpu.VMEM((2, R, DH), F32)],
        compiler_params=_params("parallel", "parallel"),
        name="hyena_fspec_a",
    )(f_a_real, hview)


def _rows2(ref, j):
    return jnp.concatenate([ref[:, 0, j, :], ref[:, 1, j, :]], axis=0)


def _conv_a_kernel(f_ref, z_ref, o_ref):
    f = f_ref[...]
    for j in range(NB):
        zz = jnp.concatenate([z_ref[0, :, j, :], z_ref[1, :, j, :]], axis=0).astype(BF16)
        o_ref[j] = _dot(f, zz)


def _conv_a_call(f_a, zview):
    return pl.pallas_call(
        _conv_a_kernel,
        grid=(2, R // NB),
        in_specs=[pl.BlockSpec((2 * R, R), lambda p, j: (0, 0)),
                  pl.BlockSpec((2, None, HALF, NB, DH), lambda p, j: (0, p, 0, j, 0))],
        out_specs=pl.BlockSpec((None, NB, 2 * R, DH), lambda p, j: (p, j, 0, 0)),
        out_shape=jax.ShapeDtypeStruct((2, R, 2 * R, DH), F32),
        compiler_params=_params("parallel", "parallel"),
        name="hyena_conv_a",
    )(f_a, zview)


def _conv_c_kernel(wr_ref, wi_ref, tr_ref, ti_ref, af_ref, a_ref, o_ref, x_ref):
    wr, wi = wr_ref[...], wi_ref[...]
    for kk in range(NB):
        tr, ti = tr_ref[kk:kk + 1, :], ti_ref[kk:kk + 1, :]
        mr = wr * tr - wi * ti
        mi = wr * ti + wi * tr
        mb = jnp.concatenate([jnp.concatenate([mr, -mi], axis=1), jnp.concatenate([mi, mr], axis=1)], axis=0)
        m_hi, m_lo = _split_bf16(mb)
        x_ref[kk % 2] = _rows2(af_ref, kk)
        f_hi, f_lo = _split_bf16(x_ref[kk % 2])
        h = (_dot(m_hi, f_hi) + (_dot(m_hi, f_lo) + _dot(m_lo, f_hi))) * (1.0 / NFFT)
        hr, hi = h[0:R], h[R:2 * R]
        a = jnp.concatenate([_rows2(a_ref.at[0], kk), _rows2(a_ref.at[1], kk)], axis=1).astype(BF16)
        x = _dot(m_hi, a)
        ys = []
        for p in range(2):
            xr, xi = x[0:R, p * DH:(p + 1) * DH], x[R:2 * R, p * DH:(p + 1) * DH]
            ys.append(jnp.concatenate([xr * hr - xi * hi, xr * hi + xi * hr], axis=0))
        b = _dot(mb.T.astype(BF16), jnp.concatenate(ys, axis=1).astype(BF16))
        o_ref[0, kk] = b[:, 0:DH]
        o_ref[1, kk] = b[:, DH:2 * DH]


def _conv_c_call(w_re, w_im, tw_re, tw_im, af_view, order, a_view):
    full = pl.BlockSpec((R, R), lambda k: (0, 0))
    trow = pl.BlockSpec((NB, R), lambda k: (k, 0))
    return pl.pallas_call(
        _conv_c_kernel,
        grid=(R // NB,),
        in_specs=[full, full, trow, trow,
                  pl.BlockSpec((None, R, 2, NB, DH), lambda k: (order, 0, 0, k, 0)),
                  pl.BlockSpec((2, R, 2, NB, DH), lambda k: (0, 0, 0, k, 0))],
        out_specs=pl.BlockSpec((2, NB, 2 * R, DH), lambda k: (0, k, 0, 0)),
        out_shape=jax.ShapeDtypeStruct((2, R, 2 * R, DH), F32),
        scratch_shapes=[pltpu.VMEM((2, 2 * R, DH), F32)],
        compiler_params=_params("parallel"),
        name="hyena_conv_c",
    )(w_re, w_im, tw_re, tw_im, af_view, a_view)


def _conv_out_kernel(first_order, g_ref, f_ref, b_ref, z_ref, gate_ref, d_ref, zo_ref, *next_a):
    g, d = g_ref[...], d_ref[...]
    for j in range(NB):
        y = _dot(g, _rows2(b_ref, j).astype(BF16))
        zn = []
        for bi in range(2):
            zc = z_ref[bi, :, j, :] if first_order else z_ref[bi, j]
            zn.append(gate_ref[bi, :, j, :] * (y[bi * HALF:(bi + 1) * HALF] + d * zc))
            zo_ref[bi, j] = zn[bi]
        if first_order:
            next_a[0][j] = _dot(f_ref[...], jnp.concatenate(zn, axis=0).astype(BF16))


def _conv_out_call(g_a, f_a, b_view, z_in, gate_view, drow, first_order):
    pm = pl.BlockSpec((2, None, HALF, NB, DH), lambda p, j: (0, p, 0, j, 0))
    nm = pl.BlockSpec((2, None, NB, HALF, DH), lambda p, j: (0, p, j, 0, 0))
    out_specs = [nm]
    out_shape = [jax.ShapeDtypeStruct((2, 2, R, HALF, DH), F32)]
    if first_order:
        out_specs.append(pl.BlockSpec((None, NB, 2 * R, DH), lambda p, j: (p, j, 0, 0)))
        out_shape.append(jax.ShapeDtypeStruct((2, R, 2 * R, DH), F32))
    return pl.pallas_call(
        functools.partial(_conv_out_kernel, first_order),
        grid=(2, R // NB),
        in_specs=[pl.BlockSpec((R, 2 * R), lambda p, j: (0, 0)), pl.BlockSpec((2 * R, R), lambda p, j: (0, 0)),
                  pl.BlockSpec((None, R, 2, NB, DH), lambda p, j: (p, 0, 0, j, 0)),
                  pm if first_order else nm, pm, pl.BlockSpec((1, DH), lambda p, j: (0, 0))],
        out_specs=out_specs,
        out_shape=out_shape,
        compiler_params=_params("parallel", "parallel"),
        name="hyena_conv_out0" if first_order else "hyena_conv_out1",
    )(g_a, f_a, b_view, z_in, gate_view, drow)


TM_OUT = NB * R


def _outproj_kernel(x_ref, ya_ref, yb_ref, wa_ref, wb_ref, g1_ref, o_ref):
    ya = jnp.concatenate([ya_ref[:, r, :] for r in range(NB)], axis=0).astype(BF16)
    mix = _dot(ya, wa_ref[...]) + _dot(yb_ref[...], wb_ref[...])
    o_ref[...] = x_ref[...] + g1_ref[...] * mix


def _outproj_call(x2, ya_nm, yb, wa, wb, g1):
    tm = TM_OUT
    tiles = S // tm
    row = lambda i: (i, 0)
    return pl.pallas_call(
        _outproj_kernel,
        grid=(T // tm,),
        in_specs=[pl.BlockSpec((tm, D), row),
                  pl.BlockSpec((None, R, NB, DH), lambda i: (i // tiles, 0, i % tiles, 0)),
                  pl.BlockSpec((tm, DD), row),
                  pl.BlockSpec((DH, D), lambda i: (0, 0)), pl.BlockSpec((DD, D), lambda i: (0, 0)),
                  pl.BlockSpec((None, 1, D), lambda i: (i // tiles, 0, 0))],
        out_specs=pl.BlockSpec((tm, D), row),
        out_shape=jax.ShapeDtypeStruct((T, D), F32),
        compiler_params=_params("parallel"),
        name="outproj",
    )(x2, ya_nm, yb, wa, wb, g1)


TM_FFN = 512
FH = 16
CK = 256
assert D_FF % CK == 0


def _ffn_kernel(final, xp_ref, xm_ref, xn_ref, g_ref, sh_ref, sc_ref, g2_ref, wup_ref, cw_ref, cb_ref, wdn_ref,
                fg_ref, o_ref, hn_ref, ug_ref, uu_ref, act_ref):
    i = pl.program_id(0)
    tiles = S // TM_FFN
    first = (i % tiles) == 0
    last = (i % tiles) == tiles - 1
    g, sh, sc = g_ref[...], sh_ref[...], sc_ref[...]
    xm = xm_ref[...]
    hn_ref[0:FH, :] = jnp.where(first, 0.0, _rms_mod(xp_ref[...], g, sh, sc)).astype(BF16)
    hn_ref[FH:FH + TM_FFN, :] = _rms_mod(xm, g, sh, sc).astype(BF16)
    hn_ref[FH + TM_FFN:2 * FH + TM_FFN, :] = jnp.where(last, 0.0, _rms_mod(xn_ref[...], g, sh, sc)).astype(BF16)
    hn = hn_ref[...]

    def conv(u_ref, c0):
        w = cw_ref[:, c0:c0 + CK]
        return (w[0:1] * u_ref[FH - 1:FH - 1 + TM_FFN, :] + w[1:2] * u_ref[FH:FH + TM_FFN, :]
                + w[2:3] * u_ref[FH + 1:FH + 1 + TM_FFN, :] + cb_ref[:, c0:c0 + CK])

    for c in range(D_FF // CK):
        c0 = c * CK
        ug, uu = ug_ref.at[c % 2], uu_ref.at[c % 2]
        ug[...] = _dot(hn, wup_ref[:, c0:c0 + CK])
        uu[...] = _dot(hn, wup_ref[:, D_FF + c0:D_FF + c0 + CK])
        act_ref[:, c0:c0 + CK] = (_silu(conv(ug, c0)) * conv(uu, D_FF + c0)).astype(BF16)
    out = xm + g2_ref[...] * _dot(act_ref[...], wdn_ref[...])
    if final:
        out = (out * lax.rsqrt(jnp.mean(out * out, axis=-1, keepdims=True) + EPS)) * fg_ref[...]
    o_ref[...] = out


def _ffn_call(x2, g, sh, sc, g2, wup, cw, cb, wdn, fg, final):
    tm = TM_FFN
    tiles = S // tm
    nh = tm // FH
    row = lambda i: (i, 0)
    bvec = pl.BlockSpec((None, 1, D), lambda i: (i // tiles, 0, 0))
    full = lambda shape: pl.BlockSpec(shape, lambda i: (0,) * len(shape))
    return pl.pallas_call(
        functools.partial(_ffn_kernel, final),
        grid=(T // tm,),
        in_specs=[
            pl.BlockSpec((FH, D), lambda i: (jnp.maximum(i * nh - 1, 0), 0)),
            pl.BlockSpec((tm, D), row),
            pl.BlockSpec((FH, D), lambda i: (jnp.minimum((i + 1) * nh, T // FH - 1), 0)),
            full((1, D)), bvec, bvec, bvec,
            full((D, 2 * D_FF)), full((3, 2 * D_FF)), full((1, 2 * D_FF)), full((D_FF, D)), full((1, D)),
        ],
        out_specs=pl.BlockSpec((tm, D), row),
        out_shape=jax.ShapeDtypeStruct((T, D), F32),
        scratch_shapes=[pltpu.VMEM((tm + 2 * FH, D), BF16), pltpu.VMEM((2, tm + 2 * FH, CK), F32),
                        pltpu.VMEM((2, tm + 2 * FH, CK), F32), pltpu.VMEM((tm, D_FF), BF16)],
        compiler_params=_params("parallel"),
        name="convffn_final" if final else "convffn",
    )(x2, x2, x2, g, sh, sc, g2, wup, cw, cb, wdn, fg)


TM_SGU = 512


def _sgu_kernel(x_ref, g_ref, sh_ref, sc_ref, g1_ref, win_ref, bin_ref, lng_ref, lnb_ref, ws_ref, bs_ref, wout_ref,
                o_ref, s_ref):
    x = x_ref[...]
    hn = _rms_mod(x, g_ref[...], sh_ref[...], sc_ref[...]).astype(BF16)
    pre = _dot(hn, win_ref[...]) + bin_ref[...]
    act = 0.5 * pre * (1.0 + lax.erf(pre * (2.0 ** -0.5)))
    u = act[:, 0:D]
    v = act[:, D:2 * D]
    mu = jnp.mean(v, axis=-1, keepdims=True)
    vc = v - mu
    v = (vc * lax.rsqrt(jnp.mean(vc * vc, axis=-1, keepdims=True) + EPS)) * lng_ref[...] + lnb_ref[...]
    vb = v.astype(BF16)
    cg = D // GROUPS
    for n in range(TM_SGU // CHUNK):
        for gi in range(GROUPS):
            s_ref[n * CHUNK:(n + 1) * CHUNK, gi * cg:(gi + 1) * cg] = (
                _dot(ws_ref[gi], vb[n * CHUNK:(n + 1) * CHUNK, gi * cg:(gi + 1) * cg])
                + bs_ref[:, gi * cg:(gi + 1) * cg])
    gated = (u * s_ref[...]).astype(BF16)
    o_ref[...] = x + g1_ref[...] * _dot(gated, wout_ref[...])


def _sgu_call(x2, g, sh, sc, g1, win, bin_, lng, lnb, ws, bs_full, wout):
    tm = TM_SGU
    tiles = S // tm
    row = lambda i: (i, 0)
    bvec = pl.BlockSpec((None, 1, D), lambda i: (i // tiles, 0, 0))
    full = lambda shape: pl.BlockSpec(shape, lambda i: (0,) * len(shape))
    return pl.pallas_call(
        _sgu_kernel,
        grid=(T // tm,),
        in_specs=[pl.BlockSpec((tm, D), row), full((1, D)), bvec, bvec, bvec,
                  full((D, 2 * D)), full((1, 2 * D)), full((1, D)), full((1, D)),
                  full((GROUPS, CHUNK, CHUNK)), full((CHUNK, D)), full((D, D))],
        out_specs=pl.BlockSpec((tm, D), row),
        out_shape=jax.ShapeDtypeStruct((T, D), F32),
        scratch_shapes=[pltpu.VMEM((tm, D), F32)],
        compiler_params=_params("parallel"),
        name="sgu",
    )(x2, g, sh, sc, g1, win, bin_, lng, lnb, ws, bs_full, wout)


def kernel(x, c, ctx, c_ctx, mod_w, mod_b, norm_mix_g, norm_ffn_g, ffn_w_up, ffn_conv_w, ffn_conv_b, ffn_w_down,
           ab_w_in, hy_conv_w, hy_conv_b, hy_w1, hy_b1, hy_freq, hy_w2, hy_b2, hy_w3, hy_bias, da_lambda,
           da_subln_g, ab_w_out, sgu_w_in, sgu_b_in, sgu_ln_g, sgu_ln_b, sgu_w_s, sgu_b_s, sgu_w_out,
           final_norm_g):
    x2 = x.reshape(T, D)
    cc = jnp.concatenate([c, c_ctx[None, :], jnp.zeros((3, D), F32)], axis=0)
    mods = _mod_call(cc, mod_w, mod_b)

    def mod_vec(layer, k):
        return mods[layer, 0:B, k * D:(k + 1) * D].reshape(B, 1, D)

    w_in = ab_w_in[0].astype(BF16)
    n_hq = 3 * DH + DD
    cos_t, sin_t = (jnp.asarray(t) for t in _rope_tables())
    w_k = w_in[:, n_hq:n_hq + DD]
    w_vt = w_in[:, n_hq + DD:].T
    hv, hx1, hx2, q, k, vt = _inproj_call(
        x2, norm_mix_g[0][None, :], mod_vec(0, 0), mod_vec(0, 1),
        w_in[:, 0:3 * DH], jnp.concatenate([w_in[:, 3 * DH:n_hq], w_k], axis=1), w_vt,
        cos_t, sin_t, hy_conv_w[0], hy_conv_b[0][None, :])
    kc, vct = _ctxkv_call(ctx.reshape(B * CTX, D), norm_mix_g[0][None, :], mods[0, 4:5, 0:D], mods[0, 4:5, D:2 * D],
                          w_k, w_vt)
    yb = _attn_call(da_lambda[0], da_subln_g[0][:, None], q.reshape(B, S, DD), k.reshape(B, S, DD), vt,
                    kc.reshape(B, CTX, DD), vct)

    f_a, f_a_real, g_a, w_re, w_im, tw_re, tw_im = (jnp.asarray(t) for t in _dft_tables())
    w1p = jnp.pad(hy_w1[0], ((0, FILTER_EMB_PAD - FILTER_EMB), (0, 0)))
    w3r = hy_w3[0].reshape(FILTER_WIDTH, 2, 2, DH).transpose(1, 2, 0, 3)
    zpos = _filter_positions()
    filt = _filter_call(jnp.asarray(np.ascontiguousarray(zpos.T)), jnp.asarray(zpos[:, 0:1]), w1p.T,
                        hy_b1[0][:, None], hy_freq[0].T, hy_w2[0].T, hy_b2[0][:, None], w3r, _decay_rates())
    af_view = _fspec_a_call(f_a_real, filt.reshape(2, R, R, DH)).reshape(2, R, 2, R, DH)

    f_a_b, g_a_b = f_a.astype(BF16), g_a.astype(BF16)
    pm = lambda t: t.reshape(2, 2, HALF, R, DH)
    a = _conv_a_call(f_a_b, pm(hv))
    bq = _conv_c_call(w_re, w_im, tw_re, tw_im, af_view, 0, a.reshape(2, R, 2, R, DH))
    z1, a = _conv_out_call(g_a_b, f_a_b, bq.reshape(2, R, 2, R, DH), pm(hv), pm(hx1), hy_bias[0, 0][None, :], True)
    bq = _conv_c_call(w_re, w_im, tw_re, tw_im, af_view, 1, a.reshape(2, R, 2, R, DH))
    (z2,) = _conv_out_call(g_a_b, f_a_b, bq.reshape(2, R, 2, R, DH), z1, pm(hx2), hy_bias[0, 1][None, :], False)

    w_out = ab_w_out[0].astype(BF16)
    x2 = _outproj_call(x2, z2.reshape(B, R, HALF, DH), yb.reshape(T, DD), w_out[0:DH], w_out[DH:], mod_vec(0, 2))
    x2 = _ffn_call(x2, norm_ffn_g[0][None, :], mod_vec(0, 3), mod_vec(0, 4), mod_vec(0, 5),
                   ffn_w_up[0].astype(BF16), ffn_conv_w[0], ffn_conv_b[0][None, :], ffn_w_down[0].astype(BF16),
                   final_norm_g[None, :], False)

    bs_full = jnp.repeat(sgu_b_s[0].T, D // GROUPS, axis=1)
    x2 = _sgu_call(x2, norm_mix_g[1][None, :], mod_vec(1, 0), mod_vec(1, 1), mod_vec(1, 2),
                   sgu_w_in[0].astype(BF16), sgu_b_in[0][None, :], sgu_ln_g[0][None, :], sgu_ln_b[0][None, :],
                   sgu_w_s[0].astype(BF16), bs_full, sgu_w_out[0].astype(BF16))
    x2 = _ffn_call(x2, norm_ffn_g[1][None, :], mod_vec(1, 3), mod_vec(1, 4), mod_vec(1, 5),
                   ffn_w_up[1].astype(BF16), ffn_conv_w[1], ffn_conv_b[1][None, :], ffn_w_down[1].astype(BF16),
                   final_norm_g[None, :], True)
    return x2.reshape(B, S, D)
```

```python
import functools
import math

import numpy as np
import jax
import jax.numpy as jnp
from jax import lax
from jax.experimental import pallas as pl
from jax.experimental.pallas import tpu as pltpu

D = 1024
B = 4
S = 8192
T = B * S
GRID_W = 64
CTX = 256
EPS = 1e-6
DH = 512
DD = 512
HEADS = 4
FILTER_EMB = 33
FILTER_EMB_PAD = 64
FILTER_WIDTH = 64
D_FF = 2816
CHUNK = 128
GROUPS = 8

NFFT = 2 * S
R = 128
HALF = R // 2
WCOL = R * DH

F32 = jnp.float32
BF16 = jnp.bfloat16
HI = lax.Precision.HIGHEST

VMEM_LIMIT = 56 * 1024 * 1024


def _params(*sem):
    return pltpu.CompilerParams(dimension_semantics=sem, vmem_limit_bytes=VMEM_LIMIT)


def _rms_mod(x, g, sh, sc):
    y = x * lax.rsqrt(jnp.mean(x * x, axis=-1, keepdims=True) + EPS)
    return (y * g) * (1.0 + sc) + sh


def _silu(x):
    return x * (1.0 / (1.0 + jnp.exp(-x)))


def _dot(a, b, **kw):
    return jnp.dot(a, b, preferred_element_type=F32, **kw)


def _mod_kernel(c_ref, w_ref, b_ref, o_ref):
    o_ref[...] = _dot(_silu(c_ref[...]), w_ref[...], precision=HI) + b_ref[...]


def _mod_call(cc, mod_w, mod_b):
    depth = mod_w.shape[0]
    tn = 1024
    return pl.pallas_call(
        _mod_kernel,
        grid=(depth, 6 * D // tn),
        in_specs=[
            pl.BlockSpec((8, D), lambda l, j: (0, 0)),
            pl.BlockSpec((None, D, tn), lambda l, j: (l, 0, j)),
            pl.BlockSpec((None, 1, tn), lambda l, j: (l, 0, j)),
        ],
        out_specs=pl.BlockSpec((None, 8, tn), lambda l, j: (l, 0, j)),
        out_shape=jax.ShapeDtypeStruct((depth, 8, 6 * D), F32),
        compiler_params=_params("parallel", "parallel"),
        name="mod",
    )(cc, mod_w, mod_b.reshape(depth, 1, 6 * D))


TM_IN = 512
HALO = 8


def _dot_nt(a, b):
    return lax.dot_general(a, b, (((1,), (1,)), ((), ())), preferred_element_type=F32)


def _inproj_kernel(xp_ref, xm_ref, xn_ref, g_ref, sh_ref, sc_ref, why_ref, wqk_ref, wvt_ref, cos_ref, sin_ref,
                   cw_ref, cb_ref, hv_ref, hx1_ref, hx2_ref, q_ref, k_ref, vt_ref, buf_ref):
    i = pl.program_id(0)
    tiles = S // TM_IN
    first = (i % tiles) == 0
    last = (i % tiles) == tiles - 1
    g, sh, sc = g_ref[...], sh_ref[...], sc_ref[...]
    hn = _rms_mod(xm_ref[...], g, sh, sc).astype(BF16)
    hp = jnp.where(first, 0.0, _rms_mod(xp_ref[...], g, sh, sc)).astype(BF16)
    hx = jnp.where(last, 0.0, _rms_mod(xn_ref[...], g, sh, sc)).astype(BF16)
    why = why_ref[...]
    buf_ref[0:HALO, :] = _dot(hp, why)
    buf_ref[HALO:HALO + TM_IN, :] = _dot(hn, why)
    buf_ref[HALO + TM_IN:2 * HALO + TM_IN, :] = _dot(hx, why)
    cw = cw_ref[...]
    ext = buf_ref[...]
    prev = pltpu.roll(ext, 1, 0)[HALO:HALO + TM_IN]
    nxt = pltpu.roll(ext, TM_IN + 2 * HALO - 1, 0)[HALO:HALO + TM_IN]
    conv = cw[0:1] * prev + cw[1:2] * ext[HALO:HALO + TM_IN] + cw[2:3] * nxt + cb_ref[...]
    hv_ref[...] = conv[:, 0:DH]
    hx1_ref[...] = conv[:, DH:2 * DH]
    hx2_ref[...] = conv[:, 2 * DH:3 * DH]

    qkv = _dot(hn, wqk_ref[...])
    vt_ref[...] = _dot_nt(wvt_ref[...], hn).astype(BF16)
    cos, sin = cos_ref[...], sin_ref[...]
    lane = lax.broadcasted_iota(jnp.int32, (TM_IN, 128), 1)
    even = ((lane // 16) % 2) == 0

    def rope(xg):
        partner = jnp.where(even, pltpu.roll(xg, 112, 1), pltpu.roll(xg, 16, 1))
        return xg * cos + partner * sin

    scale = 64 ** -0.5 * math.log2(math.e)
    for h in range(HEADS):
        q_ref[:, h * 128:(h + 1) * 128] = (rope(qkv[:, h * 128:(h + 1) * 128]) * scale).astype(BF16)
        k_ref[:, h * 128:(h + 1) * 128] = rope(qkv[:, DD + h * 128:DD + (h + 1) * 128]).astype(BF16)


def _inproj_call(x2, g, sh, sc, why, wqk, wvt, cos_t, sin_t, cw, cb):
    tm = TM_IN
    tiles = S // tm
    nh = tm // HALO
    row = lambda i: (i, 0)
    bvec = pl.BlockSpec((None, 1, D), lambda i: (i // tiles, 0, 0))
    full = lambda shape: pl.BlockSpec(shape, lambda i: (0,) * len(shape))
    return pl.pallas_call(
        _inproj_kernel,
        grid=(T // tm,),
        in_specs=[
            pl.BlockSpec((HALO, D), lambda i: (jnp.maximum(i * nh - 1, 0), 0)),
            pl.BlockSpec((tm, D), row),
            pl.BlockSpec((HALO, D), lambda i: (jnp.minimum((i + 1) * nh, T // HALO - 1), 0)),
            full((1, D)), bvec, bvec,
            full((D, 3 * DH)), full((D, 2 * DD)), full((DD, D)),
            pl.BlockSpec((tm, 128), lambda i: (i % tiles, 0)),
            pl.BlockSpec((tm, 128), lambda i: (i % tiles, 0)),
            full((3, 3 * DH)), full((1, 3 * DH)),
        ],
        out_specs=[pl.BlockSpec((tm, DH), row)] * 3 + [pl.BlockSpec((tm, DD), row)] * 2
        + [pl.BlockSpec((None, DD, tm), lambda i: (i // tiles, 0, i % tiles))],
        out_shape=[jax.ShapeDtypeStruct((T, DH), F32)] * 3 + [jax.ShapeDtypeStruct((T, DD), BF16)] * 2
        + [jax.ShapeDtypeStruct((B, DD, S), BF16)],
        scratch_shapes=[pltpu.VMEM((tm + 2 * HALO, 3 * DH), F32)],
        compiler_params=_params("parallel"),
        name="inproj",
    )(x2, x2, x2, g, sh, sc, why, wqk, wvt, cos_t, sin_t, cw, cb)


def _ctxkv_kernel(x_ref, g_ref, sh_ref, sc_ref, wk_ref, wvt_ref, k_ref, vt_ref):
    hn = _rms_mod(x_ref[...], g_ref[...], sh_ref[...], sc_ref[...]).astype(BF16)
    k_ref[...] = _dot(hn, wk_ref[...]).astype(BF16)
    vt_ref[...] = _dot_nt(wvt_ref[...], hn).astype(BF16)


def _ctxkv_call(ctx2, g, sh, sc, wk, wvt):
    tm = CTX
    full = lambda shape: pl.BlockSpec(shape, lambda i: (0,) * len(shape))
    return pl.pallas_call(
        _ctxkv_kernel,
        grid=(B * CTX // tm,),
        in_specs=[pl.BlockSpec((tm, D), lambda i: (i, 0)), full((1, D)), full((1, D)), full((1, D)),
                  full((D, DD)), full((DD, D))],
        out_specs=[pl.BlockSpec((tm, DD), lambda i: (i, 0)), pl.BlockSpec((None, DD, tm), lambda i: (i, 0, 0))],
        out_shape=[jax.ShapeDtypeStruct((B * CTX, DD), BF16), jax.ShapeDtypeStruct((B, DD, CTX), BF16)],
        compiler_params=_params("parallel"),
        name="ctxkv",
    )(ctx2, g, sh, sc, wk, wvt)


TQ = 256
TK = 1024
ONES_ROWS = 16
LAM_INIT0 = 0.8 - 0.6 * math.exp(-0.3 * 0)


def _attn_kernel(lam_ref, g_ref, q_ref, k_ref, vt_ref, kc_ref, vct_ref, o_ref, sa_ref, sb_ref, acc_ref):
    lane = lax.broadcasted_iota(jnp.int32, (TQ, 128), 1)
    lo = lane < 64
    lp = lam_ref[...]
    lam = (jnp.exp(jnp.sum(lp[0:1] * lp[1:2], axis=1, keepdims=True))
           - jnp.exp(jnp.sum(lp[2:3] * lp[3:4], axis=1, keepdims=True)) + LAM_INIT0)
    nblk = S // TK

    def key_block(j):
        if j == 0:
            return kc_ref[...], vct_ref[...], 0, CTX
        return (k_ref[(j - 1) * TK:j * TK, :], vt_ref[:, (j - 1) * TK:j * TK], CTX + (j - 1) * TK, TK)

    def load_q(t):
        q = q_ref[pl.ds(pl.multiple_of(t * TQ, TQ), TQ), :]
        zero = jnp.zeros_like(q)
        return jnp.where(lo, q, zero), jnp.where(lo, zero, q)

    def scores_block(qs, j, s_ref, m8):
        kblk, _, off, n = key_block(j)
        out = []
        for mi in range(2):
            s = _dot_nt(kblk, qs[mi])
            s_ref[mi, off:off + n, :] = s
            part = jnp.max(s.reshape(n // 8, 8, TQ), axis=0)
            out.append(part if m8 is None else jnp.maximum(m8[mi], part))
        return out

    def probs_block(j, s_ref, m):
        _, vtblk, off, n = key_block(j)
        aug = jnp.concatenate([vtblk, jnp.ones((ONES_ROWS, n), BF16)], axis=0)
        for mi in range(2):
            p = jnp.exp2(s_ref[mi, off:off + n, :] - m[mi]).astype(BF16)
            pv = _dot(aug, p)
            if j == 0:
                acc_ref[mi] = pv
            else:
                acc_ref[mi] += pv

    def finish(t):
        o = (acc_ref[0, 0:128, :] / acc_ref[0, 128:129, :]
             - lam * (acc_ref[1, 0:128, :] / acc_ref[1, 128:129, :]))
        o = o * lax.rsqrt(jnp.mean(o * o, axis=0, keepdims=True) + EPS)
        o = (o * g_ref[...]) * (1.0 - LAM_INIT0)
        o_ref[pl.ds(pl.multiple_of(t * TQ, TQ), TQ), :] = o.T.astype(o_ref.dtype)

    def col_max(m8):
        return [jnp.max(m8[mi], axis=0, keepdims=True) for mi in range(2)]

    def stage(t_score, s_write, t_prob, s_read, m_read):
        qs = load_q(t_score)
        m8 = None
        for j in range(nblk + 1):
            m8 = scores_block(qs, j, s_write, m8)
            probs_block(j, s_read, m_read)
        finish(t_prob)
        return col_max(m8)

    qs0 = load_q(0)
    m8 = None
    for j in range(nblk + 1):
        m8 = scores_block(qs0, j, sa_ref, m8)
    m_a0 = col_max(m8)

    ntile = S // TQ

    def body(t, m):
        t_next = jnp.minimum(t + 1, ntile - 1)
        return lax.cond(t % 2 == 0,
                        lambda m: tuple(stage(t_next, sb_ref, t, sa_ref, m)),
                        lambda m: tuple(stage(t_next, sa_ref, t, sb_ref, m)), m)

    lax.fori_loop(0, ntile, body, tuple(m_a0))


def _attn_call(lam_p, subln_g_col, q, k, vt, kc, vct):
    qspec = pl.BlockSpec((None, S, 128), lambda b, h: (b, 0, h))
    return pl.pallas_call(
        _attn_kernel,
        grid=(B, HEADS),
        in_specs=[pl.BlockSpec((4, 64), lambda b, h: (0, 0)), pl.BlockSpec((128, 1), lambda b, h: (0, 0)),
                  qspec, qspec,
                  pl.BlockSpec((None, 128, S), lambda b, h: (b, h, 0)),
                  pl.BlockSpec((None, CTX, 128), lambda b, h: (b, 0, h)),
                  pl.BlockSpec((None, 128, CTX), lambda b, h: (b, h, 0))],
        out_specs=qspec,
        out_shape=jax.ShapeDtypeStruct((B, S, DD), BF16),
        scratch_shapes=[pltpu.VMEM((2, CTX + S, TQ), F32), pltpu.VMEM((2, CTX + S, TQ), F32),
                        pltpu.VMEM((2, 128 + ONES_ROWS, TQ), F32)],
        compiler_params=_params("parallel", "parallel"),
        name="diffattn",
    )(lam_p, subln_g_col, q, k, vt, kc, vct)


def _cplx_block(m):
    return np.block([[m.real, -m.imag], [m.imag, m.real]])


@functools.lru_cache(maxsize=None)
def _dft_tables():
    k = np.arange(R)
    w_r = np.exp(-2j * np.pi * np.outer(k, k) / R)
    f_a = _cplx_block(w_r[:, :HALF])
    f_a_real = np.concatenate([w_r.real, w_r.imag], axis=0)
    g_a = _cplx_block(np.conj(w_r)[:HALF, :])
    tw = np.exp(-2j * np.pi * np.outer(k, k) / NFFT)
    f32 = lambda a: np.ascontiguousarray(a, dtype=np.float32)
    return (f32(f_a), f32(f_a_real), f32(g_a), f32(w_r.real), f32(w_r.imag), f32(tw.real), f32(tw.imag))


@functools.lru_cache(maxsize=None)
def _filter_positions():
    idx = np.arange(NFFT)
    pos = np.where(idx < S, idx, NFFT - idx) % S
    bands = (FILTER_EMB - 1) // 2
    t = np.linspace(0.0, 1.0, S)[:, None]
    ang = (2.0 * math.pi / S) * np.arange(S)[:, None] * np.linspace(1e-4, bands - 1, bands)[None, :]
    z = np.concatenate([t, np.cos(ang), -np.sin(ang)], axis=-1)
    z = np.pad(z, ((0, 0), (0, FILTER_EMB_PAD - FILTER_EMB)))
    return np.ascontiguousarray(z[pos], dtype=np.float32)


def _decay_rates():
    return jnp.abs(jnp.linspace(math.log(1e-2) / 1.5, math.log(1e-2) / 0.3, DH, dtype=F32))[None, :]


@functools.lru_cache(maxsize=None)
def _rope_tables():
    rows = S // GRID_W
    row = np.repeat(np.arange(rows), GRID_W).astype(np.float64)
    col = np.tile(np.arange(GRID_W), rows).astype(np.float64)
    m = 16
    inv = 10000.0 ** (-np.arange(m) / m)
    ang_r = row[:, None] * inv[None, :]
    ang_c = col[:, None] * inv[None, :]
    cos = np.concatenate([np.cos(ang_r)] * 2 + [np.cos(ang_c)] * 2, axis=1)
    sin = np.concatenate([-np.sin(ang_r), np.sin(ang_r), -np.sin(ang_c), np.sin(ang_c)], axis=1)
    f32 = lambda a: np.ascontiguousarray(np.concatenate([a, a], axis=1), dtype=np.float32)
    return f32(cos), f32(sin)


TR_F = 512


def _filter_kernel(zt_ref, t_ref, w1t_ref, b1_ref, fr_ref, w2t_ref, b2_ref, w3_ref, dec_ref, o_ref):
    i = pl.program_id(0)
    fr = fr_ref[...]
    h = jnp.sin(fr[:, 0:1] * (_dot3(w1t_ref[...], zt_ref[...]) + b1_ref[...]))
    h = jnp.sin(fr[:, 1:2] * (_dot3(w2t_ref[...], h) + b2_ref[...]))
    h = h.T
    window = jnp.exp(-t_ref[...] * dec_ref[...]) + 0.05
    ridx = i * TR_F + lax.broadcasted_iota(jnp.int32, (TR_F, 1), 0)
    for o in range(2):
        o_ref[o] = jnp.where(ridx == S, 0.0, _dot3(h, w3_ref[o]) * window)


def _filter_call(zt, tcol, w1t, b1c, freqc, w2t, b2c, w3r, dec):
    full = lambda shape: pl.BlockSpec(shape, lambda i: (0,) * len(shape))
    half_tiles = S // TR_F
    return pl.pallas_call(
        _filter_kernel,
        grid=(NFFT // TR_F,),
        in_specs=[pl.BlockSpec((FILTER_EMB_PAD, TR_F), lambda i: (0, i)), pl.BlockSpec((TR_F, 1), lambda i: (i, 0)),
                  full((FILTER_WIDTH, FILTER_EMB_PAD)), full((FILTER_WIDTH, 1)), full((FILTER_WIDTH, 2)),
                  full((FILTER_WIDTH, FILTER_WIDTH)), full((FILTER_WIDTH, 1)),
                  pl.BlockSpec((2, None, FILTER_WIDTH, DH), lambda i: (0, i // half_tiles, 0, 0)),
                  full((1, DH))],
        out_specs=pl.BlockSpec((2, TR_F, DH), lambda i: (0, i, 0)),
        out_shape=jax.ShapeDtypeStruct((2, NFFT, DH), F32),
        compiler_params=_params("parallel"),
        name="hyena_filter",
    )(zt, tcol, w1t, b1c, freqc, w2t, b2c, w3r, dec)


NB = 8


def _split_bf16(a):
    hi = a.astype(BF16)
    return hi, (a - hi.astype(F32)).astype(BF16)


def _dot3(a, b):
    a_hi, a_lo = _split_bf16(a)
    b_hi, b_lo = _split_bf16(b)
    return _dot(a_hi, b_hi) + (_dot(a_hi, b_lo) + _dot(a_lo, b_hi))


def _fspec_a_kernel(f_ref, h_ref, o_ref, x_ref):
    f_hi, f_lo = _split_bf16(f_ref[...])
    for j in range(NB):
        x_ref[j % 2] = h_ref[:, j, :]
        x_hi, x_lo = _split_bf16(x_ref[j % 2])
        o_ref[j] = _dot(f_hi, x_hi) + (_dot(f_hi, x_lo) + _dot(f_lo, x_hi))


def _fspec_a_call(f_a_real, hview):
    return pl.pallas_call(
        _fspec_a_kernel,
        grid=(2, R // NB),
        in_specs=[pl.BlockSpec((2 * R, R), lambda o, j: (0, 0)),
                  pl.BlockSpec((None, R, NB, DH), lambda o, j: (o, 0, j, 0))],
        out_specs=pl.BlockSpec((None, NB, 2 * R, DH), lambda o, j: (o, j, 0, 0)),
        out_shape=jax.ShapeDtypeStruct((2, R, 2 * R, DH), F32),
        scratch_shapes=[pltpu.VMEM((2, R, DH), F32)],
        compiler_params=_params("parallel", "parallel"),
        name="hyena_fspec_a",
    )(f_a_real, hview)


def _rows2(ref, j):
    return jnp.concatenate([ref[:, 0, j, :], ref[:, 1, j, :]], axis=0)


def _conv_a_kernel(f_ref, z_ref, o_ref):
    f = f_ref[...]
    for j in range(NB):
        zz = jnp.concatenate([z_ref[0, :, j, :], z_ref[1, :, j, :]], axis=0).astype(BF16)
        o_ref[j] = _dot(f, zz)


def _conv_a_call(f_a, zview):
    return pl.pallas_call(
        _conv_a_kernel,
        grid=(2, R // NB),
        in_specs=[pl.BlockSpec((2 * R, R), lambda p, j: (0, 0)),
                  pl.BlockSpec((2, None, HALF, NB, DH), lambda p, j: (0, p, 0, j, 0))],
        out_specs=pl.BlockSpec((None, NB, 2 * R, DH), lambda p, j: (p, j, 0, 0)),
        out_shape=jax.ShapeDtypeStruct((2, R, 2 * R, DH), F32),
        compiler_params=_params("parallel", "parallel"),
        name="hyena_conv_a",
    )(f_a, zview)


def _conv_c_kernel(wr_ref, wi_ref, tr_ref, ti_ref, af_ref, a_ref, o_ref, x_ref):
    wr, wi = wr_ref[...], wi_ref[...]
    for kk in range(NB):
        tr, ti = tr_ref[kk:kk + 1, :], ti_ref[kk:kk + 1, :]
        mr = wr * tr - wi * ti
        mi = wr * ti + wi * tr
        mb = jnp.concatenate([jnp.concatenate([mr, -mi], axis=1), jnp.concatenate([mi, mr], axis=1)], axis=0)
        m_hi, m_lo = _split_bf16(mb)
        x_ref[kk % 2] = _rows2(af_ref, kk)
        f_hi, f_lo = _split_bf16(x_ref[kk % 2])
        h = (_dot(m_hi, f_hi) + (_dot(m_hi, f_lo) + _dot(m_lo, f_hi))) * (1.0 / NFFT)
        hr, hi = h[0:R], h[R:2 * R]
        a = jnp.concatenate([_rows2(a_ref.at[0], kk), _rows2(a_ref.at[1], kk)], axis=1).astype(BF16)
        x = _dot(m_hi, a)
        ys = []
        for p in range(2):
            xr, xi = x[0:R, p * DH:(p + 1) * DH], x[R:2 * R, p * DH:(p + 1) * DH]
            ys.append(jnp.concatenate([xr * hr - xi * hi, xr * hi + xi * hr], axis=0))
        b = _dot(mb.T.astype(BF16), jnp.concatenate(ys, axis=1).astype(BF16))
        o_ref[0, kk] = b[:, 0:DH]
        o_ref[1, kk] = b[:, DH:2 * DH]


def _conv_c_call(w_re, w_im, tw_re, tw_im, af_view, order, a_view):
    full = pl.BlockSpec((R, R), lambda k: (0, 0))
    trow = pl.BlockSpec((NB, R), lambda k: (k, 0))
    return pl.pallas_call(
        _conv_c_kernel,
        grid=(R // NB,),
        in_specs=[full, full, trow, trow,
                  pl.BlockSpec((None, R, 2, NB, DH), lambda k: (order, 0, 0, k, 0)),
                  pl.BlockSpec((2, R, 2, NB, DH), lambda k: (0, 0, 0, k, 0))],
        out_specs=pl.BlockSpec((2, NB, 2 * R, DH), lambda k: (0, k, 0, 0)),
        out_shape=jax.ShapeDtypeStruct((2, R, 2 * R, DH), F32),
        scratch_shapes=[pltpu.VMEM((2, 2 * R, DH), F32)],
        compiler_params=_params("parallel"),
        name="hyena_conv_c",
    )(w_re, w_im, tw_re, tw_im, af_view, a_view)


def _conv_out_kernel(first_order, g_ref, f_ref, b_ref, z_ref, gate_ref, d_ref, zo_ref, *next_a):
    g, d = g_ref[...], d_ref[...]
    for j in range(NB):
        y = _dot(g, _rows2(b_ref, j).astype(BF16))
        zn = []
        for bi in range(2):
            zc = z_ref[bi, :, j, :] if first_order else z_ref[bi, j]
            zn.append(gate_ref[bi, :, j, :] * (y[bi * HALF:(bi + 1) * HALF] + d * zc))
            zo_ref[bi, j] = zn[bi]
        if first_order:
            next_a[0][j] = _dot(f_ref[...], jnp.concatenate(zn, axis=0).astype(BF16))


def _conv_out_call(g_a, f_a, b_view, z_in, gate_view, drow, first_order):
    pm = pl.BlockSpec((2, None, HALF, NB, DH), lambda p, j: (0, p, 0, j, 0))
    nm = pl.BlockSpec((2, None, NB, HALF, DH), lambda p, j: (0, p, j, 0, 0))
    out_specs = [nm]
    out_shape = [jax.ShapeDtypeStruct((2, 2, R, HALF, DH), F32)]
    if first_order:
        out_specs.append(pl.BlockSpec((None, NB, 2 * R, DH), lambda p, j: (p, j, 0, 0)))
        out_shape.append(jax.ShapeDtypeStruct((2, R, 2 * R, DH), F32))
    return pl.pallas_call(
        functools.partial(_conv_out_kernel, first_order),
        grid=(2, R // NB),
        in_specs=[pl.BlockSpec((R, 2 * R), lambda p, j: (0, 0)), pl.BlockSpec((2 * R, R), lambda p, j: (0, 0)),
                  pl.BlockSpec((None, R, 2, NB, DH), lambda p, j: (p, 0, 0, j, 0)),
                  pm if first_order else nm, pm, pl.BlockSpec((1, DH), lambda p, j: (0, 0))],
        out_specs=out_specs,
        out_shape=out_shape,
        compiler_params=_params("parallel", "parallel"),
        name="hyena_conv_out0" if first_order else "hyena_conv_out1",
    )(g_a, f_a, b_view, z_in, gate_view, drow)


TM_OUT = NB * R


def _outproj_kernel(x_ref, ya_ref, yb_ref, wa_ref, wb_ref, g1_ref, o_ref):
    ya = jnp.concatenate([ya_ref[:, r, :] for r in range(NB)], axis=0).astype(BF16)
    mix = _dot(ya, wa_ref[...]) + _dot(yb_ref[...], wb_ref[...])
    o_ref[...] = x_ref[...] + g1_ref[...] * mix


def _outproj_call(x2, ya_nm, yb, wa, wb, g1):
    tm = TM_OUT
    tiles = S // tm
    row = lambda i: (i, 0)
    return pl.pallas_call(
        _outproj_kernel,
        grid=(T // tm,),
        in_specs=[pl.BlockSpec((tm, D), row),
                  pl.BlockSpec((None, R, NB, DH), lambda i: (i // tiles, 0, i % tiles, 0)),
                  pl.BlockSpec((tm, DD), row),
                  pl.BlockSpec((DH, D), lambda i: (0, 0)), pl.BlockSpec((DD, D), lambda i: (0, 0)),
                  pl.BlockSpec((None, 1, D), lambda i: (i // tiles, 0, 0))],
        out_specs=pl.BlockSpec((tm, D), row),
        out_shape=jax.ShapeDtypeStruct((T, D), F32),
        compiler_params=_params("parallel"),
        name="outproj",
    )(x2, ya_nm, yb, wa, wb, g1)


TM_FFN = 1024
FH = 16
CK = 256
assert D_FF % CK == 0


def _ffn_kernel(final, xp_ref, xm_ref, xn_ref, g_ref, sh_ref, sc_ref, g2_ref, wup_ref, cw_ref, cb_ref, wdn_ref,
                fg_ref, o_ref, hn_ref, act_ref):
    i = pl.program_id(0)
    tiles = S // TM_FFN
    first = (i % tiles) == 0
    last = (i % tiles) == tiles - 1
    g, sh, sc = g_ref[...], sh_ref[...], sc_ref[...]
    xm = xm_ref[...]
    hn_ref[0:FH, :] = jnp.where(first, 0.0, _rms_mod(xp_ref[...], g, sh, sc)).astype(BF16)
    hn_ref[FH:FH + TM_FFN, :] = _rms_mod(xm, g, sh, sc).astype(BF16)
    hn_ref[FH + TM_FFN:2 * FH + TM_FFN, :] = jnp.where(last, 0.0, _rms_mod(xn_ref[...], g, sh, sc)).astype(BF16)
    hn = hn_ref[...]

    rows = TM_FFN + 2 * FH

    def conv(u, c0):
        w = cw_ref[:, c0:c0 + CK]
        prev = pltpu.roll(u, 1, 0)[FH:FH + TM_FFN]
        nxt = pltpu.roll(u, rows - 1, 0)[FH:FH + TM_FFN]
        return w[0:1] * prev + w[1:2] * u[FH:FH + TM_FFN] + w[2:3] * nxt + cb_ref[:, c0:c0 + CK]

    for c in range(D_FF // CK):
        c0 = c * CK
        ug = _dot(hn, wup_ref[:, c0:c0 + CK])
        uu = _dot(hn, wup_ref[:, D_FF + c0:D_FF + c0 + CK])
        act_ref[:, c0:c0 + CK] = (_silu(conv(ug, c0)) * conv(uu, D_FF + c0)).astype(BF16)
    out = xm + g2_ref[...] * _dot(act_ref[...], wdn_ref[...])
    if final:
        out = (out * lax.rsqrt(jnp.mean(out * out, axis=-1, keepdims=True) + EPS)) * fg_ref[...]
    o_ref[...] = out


def _ffn_call(x2, g, sh, sc, g2, wup, cw, cb, wdn, fg, final):
    tm = TM_FFN
    tiles = S // tm
    nh = tm // FH
    row = lambda i: (i, 0)
    bvec = pl.BlockSpec((None, 1, D), lambda i: (i // tiles, 0, 0))
    full = lambda shape: pl.BlockSpec(shape, lambda i: (0,) * len(shape))
    once = lambda shape: pl.BlockSpec(shape, lambda i: (0,) * len(shape), pipeline_mode=pl.Buffered(1))
    return pl.pallas_call(
        functools.partial(_ffn_kernel, final),
        grid=(T // tm,),
        in_specs=[
            pl.BlockSpec((FH, D), lambda i: (jnp.maximum(i * nh - 1, 0), 0)),
            pl.BlockSpec((tm, D), row),
            pl.BlockSpec((FH, D), lambda i: (jnp.minimum((i + 1) * nh, T // FH - 1), 0)),
            full((1, D)), bvec, bvec, bvec,
            once((D, 2 * D_FF)), full((3, 2 * D_FF)), full((1, 2 * D_FF)), once((D_FF, D)), full((1, D)),
        ],
        out_specs=pl.BlockSpec((tm, D), row),
        out_shape=jax.ShapeDtypeStruct((T, D), F32),
        scratch_shapes=[pltpu.VMEM((tm + 2 * FH, D), BF16), pltpu.VMEM((tm, D_FF), BF16)],
        compiler_params=_params("parallel"),
        name="convffn_final" if final else "convffn",
    )(x2, x2, x2, g, sh, sc, g2, wup, cw, cb, wdn, fg)


TM_SGU = 512


def _sgu_kernel(x_ref, g_ref, sh_ref, sc_ref, g1_ref, win_ref, bin_ref, lng_ref, lnb_ref, ws_ref, bs_ref, wout_ref,
                o_ref, s_ref):
    x = x_ref[...]
    hn = _rms_mod(x, g_ref[...], sh_ref[...], sc_ref[...]).astype(BF16)
    pre = _dot(hn, win_ref[...]) + bin_ref[...]
    act = 0.5 * pre * (1.0 + lax.erf(pre * (2.0 ** -0.5)))
    u = act[:, 0:D]
    v = act[:, D:2 * D]
    mu = jnp.mean(v, axis=-1, keepdims=True)
    vc = v - mu
    v = (vc * lax.rsqrt(jnp.mean(vc * vc, axis=-1, keepdims=True) + EPS)) * lng_ref[...] + lnb_ref[...]
    vb = v.astype(BF16)
    cg = D // GROUPS
    for n in range(TM_SGU // CHUNK):
        for gi in range(GROUPS):
            s_ref[n * CHUNK:(n + 1) * CHUNK, gi * cg:(gi + 1) * cg] = (
                _dot(ws_ref[gi], vb[n * CHUNK:(n + 1) * CHUNK, gi * cg:(gi + 1) * cg])
                + bs_ref[:, gi * cg:(gi + 1) * cg])
    gated = (u * s_ref[...]).astype(BF16)
    o_ref[...] = x + g1_ref[...] * _dot(gated, wout_ref[...])


def _sgu_call(x2, g, sh, sc, g1, win, bin_, lng, lnb, ws, bs_full, wout):
    tm = TM_SGU
    tiles = S // tm
    row = lambda i: (i, 0)
    bvec = pl.BlockSpec((None, 1, D), lambda i: (i // tiles, 0, 0))
    full = lambda shape: pl.BlockSpec(shape, lambda i: (0,) * len(shape))
    return pl.pallas_call(
        _sgu_kernel,
        grid=(T // tm,),
        in_specs=[pl.BlockSpec((tm, D), row), full((1, D)), bvec, bvec, bvec,
                  full((D, 2 * D)), full((1, 2 * D)), full((1, D)), full((1, D)),
                  full((GROUPS, CHUNK, CHUNK)), full((CHUNK, D)), full((D, D))],
        out_specs=pl.BlockSpec((tm, D), row),
        out_shape=jax.ShapeDtypeStruct((T, D), F32),
        scratch_shapes=[pltpu.VMEM((tm, D), F32)],
        compiler_params=_params("parallel"),
        name="sgu",
    )(x2, g, sh, sc, g1, win, bin_, lng, lnb, ws, bs_full, wout)


def kernel(x, c, ctx, c_ctx, mod_w, mod_b, norm_mix_g, norm_ffn_g, ffn_w_up, ffn_conv_w, ffn_conv_b, ffn_w_down,
           ab_w_in, hy_conv_w, hy_conv_b, hy_w1, hy_b1, hy_freq, hy_w2, hy_b2, hy_w3, hy_bias, da_lambda,
           da_subln_g, ab_w_out, sgu_w_in, sgu_b_in, sgu_ln_g, sgu_ln_b, sgu_w_s, sgu_b_s, sgu_w_out,
           final_norm_g):
    x2 = x.reshape(T, D)
    cc = jnp.concatenate([c, c_ctx[None, :], jnp.zeros((3, D), F32)], axis=0)
    mods = _mod_call(cc, mod_w, mod_b)

    def mod_vec(layer, k):
        return mods[layer, 0:B, k * D:(k + 1) * D].reshape(B, 1, D)

    w_in = ab_w_in[0].astype(BF16)
    n_hq = 3 * DH + DD
    cos_t, sin_t = (jnp.asarray(t) for t in _rope_tables())
    w_k = w_in[:, n_hq:n_hq + DD]
    w_vt = w_in[:, n_hq + DD:].T
    hv, hx1, hx2, q, k, vt = _inproj_call(
        x2, norm_mix_g[0][None, :], mod_vec(0, 0), mod_vec(0, 1),
        w_in[:, 0:3 * DH], jnp.concatenate([w_in[:, 3 * DH:n_hq], w_k], axis=1), w_vt,
        cos_t, sin_t, hy_conv_w[0], hy_conv_b[0][None, :])
    kc, vct = _ctxkv_call(ctx.reshape(B * CTX, D), norm_mix_g[0][None, :], mods[0, 4:5, 0:D], mods[0, 4:5, D:2 * D],
                          w_k, w_vt)
    yb = _attn_call(da_lambda[0], da_subln_g[0][:, None], q.reshape(B, S, DD), k.reshape(B, S, DD), vt,
                    kc.reshape(B, CTX, DD), vct)

    f_a, f_a_real, g_a, w_re, w_im, tw_re, tw_im = (jnp.asarray(t) for t in _dft_tables())
    w1p = jnp.pad(hy_w1[0], ((0, FILTER_EMB_PAD - FILTER_EMB), (0, 0)))
    w3r = hy_w3[0].reshape(FILTER_WIDTH, 2, 2, DH).transpose(1, 2, 0, 3)
    zpos = _filter_positions()
    filt = _filter_call(jnp.asarray(np.ascontiguousarray(zpos.T)), jnp.asarray(zpos[:, 0:1]), w1p.T,
                        hy_b1[0][:, None], hy_freq[0].T, hy_w2[0].T, hy_b2[0][:, None], w3r, _decay_rates())
    af_view = _fspec_a_call(f_a_real, filt.reshape(2, R, R, DH)).reshape(2, R, 2, R, DH)

    f_a_b, g_a_b = f_a.astype(BF16), g_a.astype(BF16)
    pm = lambda t: t.reshape(2, 2, HALF, R, DH)
    a = _conv_a_call(f_a_b, pm(hv))
    bq = _conv_c_call(w_re, w_im, tw_re, tw_im, af_view, 0, a.reshape(2, R, 2, R, DH))
    z1, a = _conv_out_call(g_a_b, f_a_b, bq.reshape(2, R, 2, R, DH), pm(hv), pm(hx1), hy_bias[0, 0][None, :], True)
    bq = _conv_c_call(w_re, w_im, tw_re, tw_im, af_view, 1, a.reshape(2, R, 2, R, DH))
    (z2,) = _conv_out_call(g_a_b, f_a_b, bq.reshape(2, R, 2, R, DH), z1, pm(hx2), hy_bias[0, 1][None, :], False)

    w_out = ab_w_out[0].astype(BF16)
    x2 = _outproj_call(x2, z2.reshape(B, R, HALF, DH), yb.reshape(T, DD), w_out[0:DH], w_out[DH:], mod_vec(0, 2))
    x2 = _ffn_call(x2, norm_ffn_g[0][None, :], mod_vec(0, 3), mod_vec(0, 4), mod_vec(0, 5),
                   ffn_w_up[0].astype(BF16), ffn_conv_w[0], ffn_conv_b[0][None, :], ffn_w_down[0].astype(BF16),
                   final_norm_g[None, :], False)

    bs_full = jnp.repeat(sgu_b_s[0].T, D // GROUPS, axis=1)
    x2 = _sgu_call(x2, norm_mix_g[1][None, :], mod_vec(1, 0), mod_vec(1, 1), mod_vec(1, 2),
                   sgu_w_in[0].astype(BF16), sgu_b_in[0][None, :], sgu_ln_g[0][None, :], sgu_ln_b[0][None, :],
                   sgu_w_s[0].astype(BF16), bs_full, sgu_w_out[0].astype(BF16))
    x2 = _ffn_call(x2, norm_ffn_g[1][None, :], mod_vec(1, 3), mod_vec(1, 4), mod_vec(1, 5),
                   ffn_w_up[1].astype(BF16), ffn_conv_w[1], ffn_conv_b[1][None, :], ffn_w_down[1].astype(BF16),
                   final_norm_g[None, :], True)
    return x2.reshape(B, S, D)
```

```python
import functools
import math

import numpy as np
import jax
import jax.numpy as jnp
from jax import lax
from jax.experimental import pallas as pl
from jax.experimental.pallas import tpu as pltpu

D = 1024
B = 4
S = 8192
T = B * S
GRID_W = 64
CTX = 256
EPS = 1e-6
DH = 512
DD = 512
HEADS = 4
FILTER_EMB = 33
FILTER_EMB_PAD = 64
FILTER_WIDTH = 64
D_FF = 2816
CHUNK = 128
GROUPS = 8

NFFT = 2 * S
R = 128
HALF = R // 2
WCOL = R * DH

F32 = jnp.float32
BF16 = jnp.bfloat16
HI = lax.Precision.HIGHEST

VMEM_LIMIT = 56 * 1024 * 1024


def _params(*sem):
    return pltpu.CompilerParams(dimension_semantics=sem, vmem_limit_bytes=VMEM_LIMIT)


def _rms_mod(x, g, sh, sc):
    y = x * lax.rsqrt(jnp.mean(x * x, axis=-1, keepdims=True) + EPS)
    return (y * g) * (1.0 + sc) + sh


def _silu(x):
    return x * (1.0 / (1.0 + jnp.exp(-x)))


def _dot(a, b, **kw):
    return jnp.dot(a, b, preferred_element_type=F32, **kw)


def _mod_kernel(c_ref, w_ref, b_ref, o_ref):
    o_ref[...] = _dot(_silu(c_ref[...]), w_ref[...], precision=HI) + b_ref[...]


def _mod_call(cc, mod_w, mod_b):
    depth = mod_w.shape[0]
    tn = 1024
    return pl.pallas_call(
        _mod_kernel,
        grid=(depth, 6 * D // tn),
        in_specs=[
            pl.BlockSpec((8, D), lambda l, j: (0, 0)),
            pl.BlockSpec((None, D, tn), lambda l, j: (l, 0, j)),
            pl.BlockSpec((None, 1, tn), lambda l, j: (l, 0, j)),
        ],
        out_specs=pl.BlockSpec((None, 8, tn), lambda l, j: (l, 0, j)),
        out_shape=jax.ShapeDtypeStruct((depth, 8, 6 * D), F32),
        compiler_params=_params("parallel", "parallel"),
        name="mod",
    )(cc, mod_w, mod_b.reshape(depth, 1, 6 * D))


TM_IN = 512
HALO = 8


def _dot_nt(a, b):
    return lax.dot_general(a, b, (((1,), (1,)), ((), ())), preferred_element_type=F32)


def _inproj_kernel(xp_ref, xm_ref, xn_ref, g_ref, sh_ref, sc_ref, why_ref, wqk_ref, wvt_ref, cos_ref, sin_ref,
                   cw_ref, cb_ref, hv_ref, hx1_ref, hx2_ref, q_ref, k_ref, vt_ref, buf_ref):
    i = pl.program_id(0)
    tiles = S // TM_IN
    first = (i % tiles) == 0
    last = (i % tiles) == tiles - 1
    g, sh, sc = g_ref[...], sh_ref[...], sc_ref[...]
    hn = _rms_mod(xm_ref[...], g, sh, sc).astype(BF16)
    hp = jnp.where(first, 0.0, _rms_mod(xp_ref[...], g, sh, sc)).astype(BF16)
    hx = jnp.where(last, 0.0, _rms_mod(xn_ref[...], g, sh, sc)).astype(BF16)
    why = why_ref[...]
    buf_ref[0:HALO, :] = _dot(hp, why)
    buf_ref[HALO:HALO + TM_IN, :] = _dot(hn, why)
    buf_ref[HALO + TM_IN:2 * HALO + TM_IN, :] = _dot(hx, why)
    cw = cw_ref[...]
    ext = buf_ref[...]
    prev = pltpu.roll(ext, 1, 0)[HALO:HALO + TM_IN]
    nxt = pltpu.roll(ext, TM_IN + 2 * HALO - 1, 0)[HALO:HALO + TM_IN]
    conv = cw[0:1] * prev + cw[1:2] * ext[HALO:HALO + TM_IN] + cw[2:3] * nxt + cb_ref[...]
    hv_ref[...] = conv[:, 0:DH]
    hx1_ref[...] = conv[:, DH:2 * DH]
    hx2_ref[...] = conv[:, 2 * DH:3 * DH]

    qkv = _dot(hn, wqk_ref[...])
    vt_ref[...] = _dot_nt(wvt_ref[...], hn).astype(BF16)
    cos, sin = cos_ref[...], sin_ref[...]
    lane = lax.broadcasted_iota(jnp.int32, (TM_IN, 128), 1)
    even = ((lane // 16) % 2) == 0

    def rope(xg):
        partner = jnp.where(even, pltpu.roll(xg, 112, 1), pltpu.roll(xg, 16, 1))
        return xg * cos + partner * sin

    scale = 64 ** -0.5 * math.log2(math.e)
    for h in range(HEADS):
        q_ref[:, h * 128:(h + 1) * 128] = (rope(qkv[:, h * 128:(h + 1) * 128]) * scale).astype(BF16)
        k_ref[:, h * 128:(h + 1) * 128] = rope(qkv[:, DD + h * 128:DD + (h + 1) * 128]).astype(BF16)


def _inproj_call(x2, g, sh, sc, why, wqk, wvt, cos_t, sin_t, cw, cb):
    tm = TM_IN
    tiles = S // tm
    nh = tm // HALO
    row = lambda i: (i, 0)
    bvec = pl.BlockSpec((None, 1, D), lambda i: (i // tiles, 0, 0))
    full = lambda shape: pl.BlockSpec(shape, lambda i: (0,) * len(shape))
    return pl.pallas_call(
        _inproj_kernel,
        grid=(T // tm,),
        in_specs=[
            pl.BlockSpec((HALO, D), lambda i: (jnp.maximum(i * nh - 1, 0), 0)),
            pl.BlockSpec((tm, D), row),
            pl.BlockSpec((HALO, D), lambda i: (jnp.minimum((i + 1) * nh, T // HALO - 1), 0)),
            full((1, D)), bvec, bvec,
            full((D, 3 * DH)), full((D, 2 * DD)), full((DD, D)),
            pl.BlockSpec((tm, 128), lambda i: (i % tiles, 0)),
            pl.BlockSpec((tm, 128), lambda i: (i % tiles, 0)),
            full((3, 3 * DH)), full((1, 3 * DH)),
        ],
        out_specs=[pl.BlockSpec((tm, DH), row)] * 3 + [pl.BlockSpec((tm, DD), row)] * 2
        + [pl.BlockSpec((None, DD, tm), lambda i: (i // tiles, 0, i % tiles))],
        out_shape=[jax.ShapeDtypeStruct((T, DH), F32)] * 3 + [jax.ShapeDtypeStruct((T, DD), BF16)] * 2
        + [jax.ShapeDtypeStruct((B, DD, S), BF16)],
        scratch_shapes=[pltpu.VMEM((tm + 2 * HALO, 3 * DH), F32)],
        compiler_params=_params("parallel"),
        name="inproj",
    )(x2, x2, x2, g, sh, sc, why, wqk, wvt, cos_t, sin_t, cw, cb)


def _ctxkv_kernel(x_ref, g_ref, sh_ref, sc_ref, wk_ref, wvt_ref, k_ref, vt_ref):
    hn = _rms_mod(x_ref[...], g_ref[...], sh_ref[...], sc_ref[...]).astype(BF16)
    k_ref[...] = _dot(hn, wk_ref[...]).astype(BF16)
    vt_ref[...] = _dot_nt(wvt_ref[...], hn).astype(BF16)


def _ctxkv_call(ctx2, g, sh, sc, wk, wvt):
    tm = CTX
    full = lambda shape: pl.BlockSpec(shape, lambda i: (0,) * len(shape))
    return pl.pallas_call(
        _ctxkv_kernel,
        grid=(B * CTX // tm,),
        in_specs=[pl.BlockSpec((tm, D), lambda i: (i, 0)), full((1, D)), full((1, D)), full((1, D)),
                  full((D, DD)), full((DD, D))],
        out_specs=[pl.BlockSpec((tm, DD), lambda i: (i, 0)), pl.BlockSpec((None, DD, tm), lambda i: (i, 0, 0))],
        out_shape=[jax.ShapeDtypeStruct((B * CTX, DD), BF16), jax.ShapeDtypeStruct((B, DD, CTX), BF16)],
        compiler_params=_params("parallel"),
        name="ctxkv",
    )(ctx2, g, sh, sc, wk, wvt)


TQ = 256
TK = 256
LAM_INIT0 = 0.8 - 0.6 * math.exp(-0.3 * 0)


def _attn_kernel(lam_ref, g_ref, q_ref, k_ref, vt_ref, kc_ref, vct_ref, o_ref, sa_ref, sb_ref, acc_ref):
    lane = lax.broadcasted_iota(jnp.int32, (TQ, 128), 1)
    lo = lane < 64
    lp = lam_ref[...]
    lam = (jnp.exp(jnp.sum(lp[0:1] * lp[1:2], axis=1, keepdims=True))
           - jnp.exp(jnp.sum(lp[2:3] * lp[3:4], axis=1, keepdims=True)) + LAM_INIT0)
    nblk = S // TK

    def key_block(j):
        if j == 0:
            return kc_ref[...], vct_ref[...], 0, CTX
        return (k_ref[(j - 1) * TK:j * TK, :], vt_ref[:, (j - 1) * TK:j * TK], CTX + (j - 1) * TK, TK)

    def load_q(t):
        q = q_ref[pl.ds(pl.multiple_of(t * TQ, TQ), TQ), :]
        zero = jnp.zeros_like(q)
        return jnp.where(lo, q, zero), jnp.where(lo, zero, q)

    def scores_block(qs, j, s_ref, m8):
        kblk, _, off, n = key_block(j)
        out = []
        for mi in range(2):
            s = _dot_nt(kblk, qs[mi])
            s_ref[mi, off:off + n, :] = s
            part = jnp.max(s.reshape(n // 8, 8, TQ), axis=0)
            out.append(part if m8 is None else jnp.maximum(m8[mi], part))
        return out

    def probs_block(j, s_ref, m, l8):
        _, vtblk, off, n = key_block(j)
        out = []
        for mi in range(2):
            p = jnp.exp2(s_ref[mi, off:off + n, :] - m[mi])
            part = jnp.sum(p.reshape(n // 8, 8, TQ), axis=0)
            out.append(part if l8 is None else l8[mi] + part)
            pv = _dot(vtblk, p.astype(BF16))
            if j == 0:
                acc_ref[mi] = pv
            else:
                acc_ref[mi] += pv
        return out

    def finish(t, l8):
        l = [jnp.sum(l8[mi], axis=0, keepdims=True) for mi in range(2)]
        o = acc_ref[0] / l[0] - lam * (acc_ref[1] / l[1])
        o = o * lax.rsqrt(jnp.mean(o * o, axis=0, keepdims=True) + EPS)
        o = (o * g_ref[...]) * (1.0 - LAM_INIT0)
        o_ref[pl.ds(pl.multiple_of(t * TQ, TQ), TQ), :] = o.T.astype(o_ref.dtype)

    def col_max(m8):
        return [jnp.max(m8[mi], axis=0, keepdims=True) for mi in range(2)]

    def stage(t_score, s_write, t_prob, s_read, m_read):
        qs = load_q(t_score)
        m8 = l8 = None
        for j in range(nblk + 1):
            m8 = scores_block(qs, j, s_write, m8)
            l8 = probs_block(j, s_read, m_read, l8)
        finish(t_prob, l8)
        return col_max(m8)

    qs0 = load_q(0)
    m8 = None
    for j in range(nblk + 1):
        m8 = scores_block(qs0, j, sa_ref, m8)
    m_a0 = col_max(m8)

    ntile = S // TQ

    def body(t, m):
        t_next = jnp.minimum(t + 1, ntile - 1)
        return lax.cond(t % 2 == 0,
                        lambda m: tuple(stage(t_next, sb_ref, t, sa_ref, m)),
                        lambda m: tuple(stage(t_next, sa_ref, t, sb_ref, m)), m)

    lax.fori_loop(0, ntile, body, tuple(m_a0))


def _attn_call(lam_p, subln_g_col, q, k, vt, kc, vct):
    qspec = pl.BlockSpec((None, S, 128), lambda b, h: (b, 0, h))
    return pl.pallas_call(
        _attn_kernel,
        grid=(B, HEADS),
        in_specs=[pl.BlockSpec((4, 64), lambda b, h: (0, 0)), pl.BlockSpec((128, 1), lambda b, h: (0, 0)),
                  qspec, qspec,
                  pl.BlockSpec((None, 128, S), lambda b, h: (b, h, 0)),
                  pl.BlockSpec((None, CTX, 128), lambda b, h: (b, 0, h)),
                  pl.BlockSpec((None, 128, CTX), lambda b, h: (b, h, 0))],
        out_specs=qspec,
        out_shape=jax.ShapeDtypeStruct((B, S, DD), BF16),
        scratch_shapes=[pltpu.VMEM((2, CTX + S, TQ), F32), pltpu.VMEM((2, CTX + S, TQ), F32),
                        pltpu.VMEM((2, 128, TQ), F32)],
        compiler_params=_params("parallel", "parallel"),
        name="diffattn",
    )(lam_p, subln_g_col, q, k, vt, kc, vct)


def _cplx_block(m):
    return np.block([[m.real, -m.imag], [m.imag, m.real]])


@functools.lru_cache(maxsize=None)
def _dft_tables():
    k = np.arange(R)
    w_r = np.exp(-2j * np.pi * np.outer(k, k) / R)
    f_a = _cplx_block(w_r[:, :HALF])
    f_a_real = np.concatenate([w_r.real, w_r.imag], axis=0)
    g_a = _cplx_block(np.conj(w_r)[:HALF, :])
    tw = np.exp(-2j * np.pi * np.outer(k, k) / NFFT)
    f32 = lambda a: np.ascontiguousarray(a, dtype=np.float32)
    return (f32(f_a), f32(f_a_real), f32(g_a), f32(w_r.real), f32(w_r.imag), f32(tw.real), f32(tw.imag))


@functools.lru_cache(maxsize=None)
def _filter_positions():
    idx = np.arange(NFFT)
    pos = np.where(idx < S, idx, NFFT - idx) % S
    bands = (FILTER_EMB - 1) // 2
    t = np.linspace(0.0, 1.0, S)[:, None]
    ang = (2.0 * math.pi / S) * np.arange(S)[:, None] * np.linspace(1e-4, bands - 1, bands)[None, :]
    z = np.concatenate([t, np.cos(ang), -np.sin(ang)], axis=-1)
    z = np.pad(z, ((0, 0), (0, FILTER_EMB_PAD - FILTER_EMB)))
    return np.ascontiguousarray(z[pos], dtype=np.float32)


def _decay_rates():
    return jnp.abs(jnp.linspace(math.log(1e-2) / 1.5, math.log(1e-2) / 0.3, DH, dtype=F32))[None, :]


@functools.lru_cache(maxsize=None)
def _rope_tables():
    rows = S // GRID_W
    row = np.repeat(np.arange(rows), GRID_W).astype(np.float64)
    col = np.tile(np.arange(GRID_W), rows).astype(np.float64)
    m = 16
    inv = 10000.0 ** (-np.arange(m) / m)
    ang_r = row[:, None] * inv[None, :]
    ang_c = col[:, None] * inv[None, :]
    cos = np.concatenate([np.cos(ang_r)] * 2 + [np.cos(ang_c)] * 2, axis=1)
    sin = np.concatenate([-np.sin(ang_r), np.sin(ang_r), -np.sin(ang_c), np.sin(ang_c)], axis=1)
    f32 = lambda a: np.ascontiguousarray(np.concatenate([a, a], axis=1), dtype=np.float32)
    return f32(cos), f32(sin)


TR_F = 512


def _filter_kernel(zt_ref, t_ref, w1t_ref, b1_ref, fr_ref, w2t_ref, b2_ref, w3_ref, dec_ref, o_ref):
    i = pl.program_id(0)
    fr = fr_ref[...]
    h = jnp.sin(fr[:, 0:1] * (_dot3(w1t_ref[...], zt_ref[...]) + b1_ref[...]))
    h = jnp.sin(fr[:, 1:2] * (_dot3(w2t_ref[...], h) + b2_ref[...]))
    h = h.T
    window = jnp.exp(-t_ref[...] * dec_ref[...]) + 0.05
    ridx = i * TR_F + lax.broadcasted_iota(jnp.int32, (TR_F, 1), 0)
    for o in range(2):
        o_ref[o] = jnp.where(ridx == S, 0.0, _dot3(h, w3_ref[o]) * window)


def _filter_call(zt, tcol, w1t, b1c, freqc, w2t, b2c, w3r, dec):
    full = lambda shape: pl.BlockSpec(shape, lambda i: (0,) * len(shape))
    half_tiles = S // TR_F
    return pl.pallas_call(
        _filter_kernel,
        grid=(NFFT // TR_F,),
        in_specs=[pl.BlockSpec((FILTER_EMB_PAD, TR_F), lambda i: (0, i)), pl.BlockSpec((TR_F, 1), lambda i: (i, 0)),
                  full((FILTER_WIDTH, FILTER_EMB_PAD)), full((FILTER_WIDTH, 1)), full((FILTER_WIDTH, 2)),
                  full((FILTER_WIDTH, FILTER_WIDTH)), full((FILTER_WIDTH, 1)),
                  pl.BlockSpec((2, None, FILTER_WIDTH, DH), lambda i: (0, i // half_tiles, 0, 0)),
                  full((1, DH))],
        out_specs=pl.BlockSpec((2, TR_F, DH), lambda i: (0, i, 0)),
        out_shape=jax.ShapeDtypeStruct((2, NFFT, DH), F32),
        compiler_params=_params("parallel"),
        name="hyena_filter",
    )(zt, tcol, w1t, b1c, freqc, w2t, b2c, w3r, dec)


NB = 8


def _split_bf16(a):
    hi = a.astype(BF16)
    return hi, (a - hi.astype(F32)).astype(BF16)


def _dot3(a, b):
    a_hi, a_lo = _split_bf16(a)
    b_hi, b_lo = _split_bf16(b)
    return _dot(a_hi, b_hi) + (_dot(a_hi, b_lo) + _dot(a_lo, b_hi))


def _fspec_a_kernel(f_ref, h_ref, o_ref, x_ref):
    f_hi, f_lo = _split_bf16(f_ref[...])
    for j in range(NB):
        x_ref[j % 2] = h_ref[:, j, :]
        x_hi, x_lo = _split_bf16(x_ref[j % 2])
        o_ref[j] = _dot(f_hi, x_hi) + (_dot(f_hi, x_lo) + _dot(f_lo, x_hi))


def _fspec_a_call(f_a_real, hview):
    return pl.pallas_call(
        _fspec_a_kernel,
        grid=(2, R // NB),
        in_specs=[pl.BlockSpec((2 * R, R), lambda o, j: (0, 0)),
                  pl.BlockSpec((None, R, NB, DH), lambda o, j: (o, 0, j, 0))],
        out_specs=pl.BlockSpec((None, NB, 2 * R, DH), lambda o, j: (o, j, 0, 0)),
        out_shape=jax.ShapeDtypeStruct((2, R, 2 * R, DH), F32),
        scratch_shapes=[pltpu.VMEM((2, R, DH), F32)],
        compiler_params=_params("parallel", "parallel"),
        name="hyena_fspec_a",
    )(f_a_real, hview)


def _rows2(ref, j):
    return jnp.concatenate([ref[:, 0, j, :], ref[:, 1, j, :]], axis=0)


def _conv_a_kernel(f_ref, z_ref, o_ref):
    f = f_ref[...]
    for j in range(NB):
        zz = jnp.concatenate([z_ref[0, :, j, :], z_ref[1, :, j, :]], axis=0).astype(BF16)
        o_ref[j] = _dot(f, zz)


def _conv_a_call(f_a, zview):
    return pl.pallas_call(
        _conv_a_kernel,
        grid=(2, R // NB),
        in_specs=[pl.BlockSpec((2 * R, R), lambda p, j: (0, 0)),
                  pl.BlockSpec((2, None, HALF, NB, DH), lambda p, j: (0, p, 0, j, 0))],
        out_specs=pl.BlockSpec((None, NB, 2 * R, DH), lambda p, j: (p, j, 0, 0)),
        out_shape=jax.ShapeDtypeStruct((2, R, 2 * R, DH), F32),
        compiler_params=_params("parallel", "parallel"),
        name="hyena_conv_a",
    )(f_a, zview)


def _conv_c_kernel(wr_ref, wi_ref, tr_ref, ti_ref, af_ref, a_ref, o_ref, x_ref):
    wr, wi = wr_ref[...], wi_ref[...]
    for kk in range(NB):
        tr, ti = tr_ref[kk:kk + 1, :], ti_ref[kk:kk + 1, :]
        mr = wr * tr - wi * ti
        mi = wr * ti + wi * tr
        mb = jnp.concatenate([jnp.concatenate([mr, -mi], axis=1), jnp.concatenate([mi, mr], axis=1)], axis=0)
        m_hi, m_lo = _split_bf16(mb)
        x_ref[kk % 2] = _rows2(af_ref, kk)
        f_hi, f_lo = _split_bf16(x_ref[kk % 2])
        h = (_dot(m_hi, f_hi) + (_dot(m_hi, f_lo) + _dot(m_lo, f_hi))) * (1.0 / NFFT)
        hr, hi = h[0:R], h[R:2 * R]
        a = jnp.concatenate([_rows2(a_ref.at[0], kk), _rows2(a_ref.at[1], kk)], axis=1).astype(BF16)
        x = _dot(m_hi, a)
        ys = []
        for p in range(2):
            xr, xi = x[0:R, p * DH:(p + 1) * DH], x[R:2 * R, p * DH:(p + 1) * DH]
            ys.append(jnp.concatenate([xr * hr - xi * hi, xr * hi + xi * hr], axis=0))
        b = _dot(mb.T.astype(BF16), jnp.concatenate(ys, axis=1).astype(BF16))
        o_ref[0, kk] = b[:, 0:DH]
        o_ref[1, kk] = b[:, DH:2 * DH]


def _conv_c_call(w_re, w_im, tw_re, tw_im, af_view, order, a_view):
    full = pl.BlockSpec((R, R), lambda k: (0, 0))
    trow = pl.BlockSpec((NB, R), lambda k: (k, 0))
    return pl.pallas_call(
        _conv_c_kernel,
        grid=(R // NB,),
        in_specs=[full, full, trow, trow,
                  pl.BlockSpec((None, R, 2, NB, DH), lambda k: (order, 0, 0, k, 0)),
                  pl.BlockSpec((2, R, 2, NB, DH), lambda k: (0, 0, 0, k, 0))],
        out_specs=pl.BlockSpec((2, NB, 2 * R, DH), lambda k: (0, k, 0, 0)),
        out_shape=jax.ShapeDtypeStruct((2, R, 2 * R, DH), F32),
        scratch_shapes=[pltpu.VMEM((2, 2 * R, DH), F32)],
        compiler_params=_params("parallel"),
        name="hyena_conv_c",
    )(w_re, w_im, tw_re, tw_im, af_view, a_view)


def _conv_out_kernel(first_order, g_ref, f_ref, b_ref, z_ref, gate_ref, d_ref, zo_ref, *next_a):
    g, d = g_ref[...], d_ref[...]
    for j in range(NB):
        y = _dot(g, _rows2(b_ref, j).astype(BF16))
        zn = []
        for bi in range(2):
            zc = z_ref[bi, :, j, :] if first_order else z_ref[bi, j]
            zn.append(gate_ref[bi, :, j, :] * (y[bi * HALF:(bi + 1) * HALF] + d * zc))
            zo_ref[bi, j] = zn[bi]
        if first_order:
            next_a[0][j] = _dot(f_ref[...], jnp.concatenate(zn, axis=0).astype(BF16))


def _conv_out_call(g_a, f_a, b_view, z_in, gate_view, drow, first_order):
    pm = pl.BlockSpec((2, None, HALF, NB, DH), lambda p, j: (0, p, 0, j, 0))
    nm = pl.BlockSpec((2, None, NB, HALF, DH), lambda p, j: (0, p, j, 0, 0))
    out_specs = [nm]
    out_shape = [jax.ShapeDtypeStruct((2, 2, R, HALF, DH), F32)]
    if first_order:
        out_specs.append(pl.BlockSpec((None, NB, 2 * R, DH), lambda p, j: (p, j, 0, 0)))
        out_shape.append(jax.ShapeDtypeStruct((2, R, 2 * R, DH), F32))
    return pl.pallas_call(
        functools.partial(_conv_out_kernel, first_order),
        grid=(2, R // NB),
        in_specs=[pl.BlockSpec((R, 2 * R), lambda p, j: (0, 0)), pl.BlockSpec((2 * R, R), lambda p, j: (0, 0)),
                  pl.BlockSpec((None, R, 2, NB, DH), lambda p, j: (p, 0, 0, j, 0)),
                  pm if first_order else nm, pm, pl.BlockSpec((1, DH), lambda p, j: (0, 0))],
        out_specs=out_specs,
        out_shape=out_shape,
        compiler_params=_params("parallel", "parallel"),
        name="hyena_conv_out0" if first_order else "hyena_conv_out1",
    )(g_a, f_a, b_view, z_in, gate_view, drow)


TM_OUT = NB * R


def _outproj_kernel(x_ref, ya_ref, yb_ref, wa_ref, wb_ref, g1_ref, o_ref):
    ya = jnp.concatenate([ya_ref[:, r, :] for r in range(NB)], axis=0).astype(BF16)
    mix = _dot(ya, wa_ref[...]) + _dot(yb_ref[...], wb_ref[...])
    o_ref[...] = x_ref[...] + g1_ref[...] * mix


def _outproj_call(x2, ya_nm, yb, wa, wb, g1):
    tm = TM_OUT
    tiles = S // tm
    row = lambda i: (i, 0)
    return pl.pallas_call(
        _outproj_kernel,
        grid=(T // tm,),
        in_specs=[pl.BlockSpec((tm, D), row),
                  pl.BlockSpec((None, R, NB, DH), lambda i: (i // tiles, 0, i % tiles, 0)),
                  pl.BlockSpec((tm, DD), row),
                  pl.BlockSpec((DH, D), lambda i: (0, 0)), pl.BlockSpec((DD, D), lambda i: (0, 0)),
                  pl.BlockSpec((None, 1, D), lambda i: (i // tiles, 0, 0))],
        out_specs=pl.BlockSpec((tm, D), row),
        out_shape=jax.ShapeDtypeStruct((T, D), F32),
        compiler_params=_params("parallel"),
        name="outproj",
    )(x2, ya_nm, yb, wa, wb, g1)


TM_FFN = 1024
FH = 16
CK = 256
assert D_FF % CK == 0


def _ffn_kernel(final, xp_ref, xm_ref, xn_ref, g_ref, sh_ref, sc_ref, g2_ref, wup_ref, cw_ref, cb_ref, wdn_ref,
                fg_ref, o_ref, hn_ref, act_ref):
    i = pl.program_id(0)
    tiles = S // TM_FFN
    first = (i % tiles) == 0
    last = (i % tiles) == tiles - 1
    g, sh, sc = g_ref[...], sh_ref[...], sc_ref[...]
    xm = xm_ref[...]
    hn_ref[0:FH, :] = jnp.where(first, 0.0, _rms_mod(xp_ref[...], g, sh, sc)).astype(BF16)
    hn_ref[FH:FH + TM_FFN, :] = _rms_mod(xm, g, sh, sc).astype(BF16)
    hn_ref[FH + TM_FFN:2 * FH + TM_FFN, :] = jnp.where(last, 0.0, _rms_mod(xn_ref[...], g, sh, sc)).astype(BF16)
    hn = hn_ref[...]

    rows = TM_FFN + 2 * FH

    def conv(u, c0):
        w = cw_ref[:, c0:c0 + CK]
        prev = pltpu.roll(u, 1, 0)[FH:FH + TM_FFN]
        nxt = pltpu.roll(u, rows - 1, 0)[FH:FH + TM_FFN]
        return w[0:1] * prev + w[1:2] * u[FH:FH + TM_FFN] + w[2:3] * nxt + cb_ref[:, c0:c0 + CK]

    for c in range(D_FF // CK):
        c0 = c * CK
        ug = _dot(hn, wup_ref[:, c0:c0 + CK])
        uu = _dot(hn, wup_ref[:, D_FF + c0:D_FF + c0 + CK])
        act_ref[:, c0:c0 + CK] = (_silu(conv(ug, c0)) * conv(uu, D_FF + c0)).astype(BF16)
    out = xm + g2_ref[...] * _dot(act_ref[...], wdn_ref[...])
    if final:
        out = (out * lax.rsqrt(jnp.mean(out * out, axis=-1, keepdims=True) + EPS)) * fg_ref[...]
    o_ref[...] = out


def _ffn_call(x2, g, sh, sc, g2, wup, cw, cb, wdn, fg, final):
    tm = TM_FFN
    tiles = S // tm
    nh = tm // FH
    row = lambda i: (i, 0)
    bvec = pl.BlockSpec((None, 1, D), lambda i: (i // tiles, 0, 0))
    full = lambda shape: pl.BlockSpec(shape, lambda i: (0,) * len(shape))
    once = lambda shape: pl.BlockSpec(shape, lambda i: (0,) * len(shape), pipeline_mode=pl.Buffered(1))
    return pl.pallas_call(
        functools.partial(_ffn_kernel, final),
        grid=(T // tm,),
        in_specs=[
            pl.BlockSpec((FH, D), lambda i: (jnp.maximum(i * nh - 1, 0), 0)),
            pl.BlockSpec((tm, D), row),
            pl.BlockSpec((FH, D), lambda i: (jnp.minimum((i + 1) * nh, T // FH - 1), 0)),
            full((1, D)), bvec, bvec, bvec,
            once((D, 2 * D_FF)), full((3, 2 * D_FF)), full((1, 2 * D_FF)), once((D_FF, D)), full((1, D)),
        ],
        out_specs=pl.BlockSpec((tm, D), row),
        out_shape=jax.ShapeDtypeStruct((T, D), F32),
        scratch_shapes=[pltpu.VMEM((tm + 2 * FH, D), BF16), pltpu.VMEM((tm, D_FF), BF16)],
        compiler_params=_params("parallel"),
        name="convffn_final" if final else "convffn",
    )(x2, x2, x2, g, sh, sc, g2, wup, cw, cb, wdn, fg)


TM_SGU = 1024
SUB_SGU = 512


def _sgu_kernel(x_ref, g_ref, sh_ref, sc_ref, g1_ref, win_ref, bin_ref, lng_ref, lnb_ref, ws_ref, bs_ref, wout_ref,
                o_ref, s_ref):
    cg = D // GROUPS
    for r0 in range(0, TM_SGU, SUB_SGU):
        x = x_ref[r0:r0 + SUB_SGU, :]
        hn = _rms_mod(x, g_ref[...], sh_ref[...], sc_ref[...]).astype(BF16)
        pre = _dot(hn, win_ref[...]) + bin_ref[...]
        act = 0.5 * pre * (1.0 + lax.erf(pre * (2.0 ** -0.5)))
        u = act[:, 0:D]
        v = act[:, D:2 * D]
        mu = jnp.mean(v, axis=-1, keepdims=True)
        vc = v - mu
        v = (vc * lax.rsqrt(jnp.mean(vc * vc, axis=-1, keepdims=True) + EPS)) * lng_ref[...] + lnb_ref[...]
        vb = v.astype(BF16)
        for n in range(SUB_SGU // CHUNK):
            for gi in range(GROUPS):
                s_ref[r0 + n * CHUNK:r0 + (n + 1) * CHUNK, gi * cg:(gi + 1) * cg] = (
                    _dot(ws_ref[gi], vb[n * CHUNK:(n + 1) * CHUNK, gi * cg:(gi + 1) * cg])
                    + bs_ref[:, gi * cg:(gi + 1) * cg])
        gated = (u * s_ref[r0:r0 + SUB_SGU, :]).astype(BF16)
        o_ref[r0:r0 + SUB_SGU, :] = x + g1_ref[...] * _dot(gated, wout_ref[...])


def _sgu_call(x2, g, sh, sc, g1, win, bin_, lng, lnb, ws, bs_full, wout):
    tm = TM_SGU
    tiles = S // tm
    row = lambda i: (i, 0)
    bvec = pl.BlockSpec((None, 1, D), lambda i: (i // tiles, 0, 0))
    full = lambda shape: pl.BlockSpec(shape, lambda i: (0,) * len(shape))
    return pl.pallas_call(
        _sgu_kernel,
        grid=(T // tm,),
        in_specs=[pl.BlockSpec((tm, D), row), full((1, D)), bvec, bvec, bvec,
                  full((D, 2 * D)), full((1, 2 * D)), full((1, D)), full((1, D)),
                  full((GROUPS, CHUNK, CHUNK)), full((CHUNK, D)), full((D, D))],
        out_specs=pl.BlockSpec((tm, D), row),
        out_shape=jax.ShapeDtypeStruct((T, D), F32),
        scratch_shapes=[pltpu.VMEM((tm, D), F32)],
        compiler_params=_params("parallel"),
        name="sgu",
    )(x2, g, sh, sc, g1, win, bin_, lng, lnb, ws, bs_full, wout)


def kernel(x, c, ctx, c_ctx, mod_w, mod_b, norm_mix_g, norm_ffn_g, ffn_w_up, ffn_conv_w, ffn_conv_b, ffn_w_down,
           ab_w_in, hy_conv_w, hy_conv_b, hy_w1, hy_b1, hy_freq, hy_w2, hy_b2, hy_w3, hy_bias, da_lambda,
           da_subln_g, ab_w_out, sgu_w_in, sgu_b_in, sgu_ln_g, sgu_ln_b, sgu_w_s, sgu_b_s, sgu_w_out,
           final_norm_g):
    x2 = x.reshape(T, D)
    cc = jnp.concatenate([c, c_ctx[None, :], jnp.zeros((3, D), F32)], axis=0)
    mods = _mod_call(cc, mod_w, mod_b)

    def mod_vec(layer, k):
        return mods[layer, 0:B, k * D:(k + 1) * D].reshape(B, 1, D)

    w_in = ab_w_in[0].astype(BF16)
    n_hq = 3 * DH + DD
    cos_t, sin_t = (jnp.asarray(t) for t in _rope_tables())
    w_k = w_in[:, n_hq:n_hq + DD]
    w_vt = w_in[:, n_hq + DD:].T
    hv, hx1, hx2, q, k, vt = _inproj_call(
        x2, norm_mix_g[0][None, :], mod_vec(0, 0), mod_vec(0, 1),
        w_in[:, 0:3 * DH], jnp.concatenate([w_in[:, 3 * DH:n_hq], w_k], axis=1), w_vt,
        cos_t, sin_t, hy_conv_w[0], hy_conv_b[0][None, :])
    kc, vct = _ctxkv_call(ctx.reshape(B * CTX, D), norm_mix_g[0][None, :], mods[0, 4:5, 0:D], mods[0, 4:5, D:2 * D],
                          w_k, w_vt)
    yb = _attn_call(da_lambda[0], da_subln_g[0][:, None], q.reshape(B, S, DD), k.reshape(B, S, DD), vt,
                    kc.reshape(B, CTX, DD), vct)

    f_a, f_a_real, g_a, w_re, w_im, tw_re, tw_im = (jnp.asarray(t) for t in _dft_tables())
    w1p = jnp.pad(hy_w1[0], ((0, FILTER_EMB_PAD - FILTER_EMB), (0, 0)))
    w3r = hy_w3[0].reshape(FILTER_WIDTH, 2, 2, DH).transpose(1, 2, 0, 3)
    zpos = _filter_positions()
    filt = _filter_call(jnp.asarray(np.ascontiguousarray(zpos.T)), jnp.asarray(zpos[:, 0:1]), w1p.T,
                        hy_b1[0][:, None], hy_freq[0].T, hy_w2[0].T, hy_b2[0][:, None], w3r, _decay_rates())
    af_view = _fspec_a_call(f_a_real, filt.reshape(2, R, R, DH)).reshape(2, R, 2, R, DH)

    f_a_b, g_a_b = f_a.astype(BF16), g_a.astype(BF16)
    pm = lambda t: t.reshape(2, 2, HALF, R, DH)
    a = _conv_a_call(f_a_b, pm(hv))
    bq = _conv_c_call(w_re, w_im, tw_re, tw_im, af_view, 0, a.reshape(2, R, 2, R, DH))
    z1, a = _conv_out_call(g_a_b, f_a_b, bq.reshape(2, R, 2, R, DH), pm(hv), pm(hx1), hy_bias[0, 0][None, :], True)
    bq = _conv_c_call(w_re, w_im, tw_re, tw_im, af_view, 1, a.reshape(2, R, 2, R, DH))
    (z2,) = _conv_out_call(g_a_b, f_a_b, bq.reshape(2, R, 2, R, DH), z1, pm(hx2), hy_bias[0, 1][None, :], False)

    w_out = ab_w_out[0].astype(BF16)
    x2 = _outproj_call(x2, z2.reshape(B, R, HALF, DH), yb.reshape(T, DD), w_out[0:DH], w_out[DH:], mod_vec(0, 2))
    x2 = _ffn_call(x2, norm_ffn_g[0][None, :], mod_vec(0, 3), mod_vec(0, 4), mod_vec(0, 5),
                   ffn_w_up[0].astype(BF16), ffn_conv_w[0], ffn_conv_b[0][None, :], ffn_w_down[0].astype(BF16),
                   final_norm_g[None, :], False)

    bs_full = jnp.repeat(sgu_b_s[0].T, D // GROUPS, axis=1)
    x2 = _sgu_call(x2, norm_mix_g[1][None, :], mod_vec(1, 0), mod_vec(1, 1), mod_vec(1, 2),
                   sgu_w_in[0].astype(BF16), sgu_b_in[0][None, :], sgu_ln_g[0][None, :], sgu_ln_b[0][None, :],
                   sgu_w_s[0].astype(BF16), bs_full, sgu_w_out[0].astype(BF16))
    x2 = _ffn_call(x2, norm_ffn_g[1][None, :], mod_vec(1, 3), mod_vec(1, 4), mod_vec(1, 5),
                   ffn_w_up[1].astype(BF16), ffn_conv_w[1], ffn_conv_b[1][None, :], ffn_w_down[1].astype(BF16),
                   final_norm_g[None, :], True)
    return x2.reshape(B, S, D)
```

```python
import functools
import math

import numpy as np
import jax
import jax.numpy as jnp
from jax import lax
from jax.experimental import pallas as pl
from jax.experimental.pallas import tpu as pltpu

D = 1024
B = 4
S = 8192
T = B * S
GRID_W = 64
CTX = 256
EPS = 1e-6
DH = 512
DD = 512
HEADS = 4
FILTER_EMB = 33
FILTER_EMB_PAD = 64
FILTER_WIDTH = 64
D_FF = 2816
CHUNK = 128
GROUPS = 8

NFFT = 2 * S
R = 128
HALF = R // 2
WCOL = R * DH

F32 = jnp.float32
BF16 = jnp.bfloat16
HI = lax.Precision.HIGHEST

VMEM_LIMIT = 56 * 1024 * 1024


def _params(*sem):
    return pltpu.CompilerParams(dimension_semantics=sem, vmem_limit_bytes=VMEM_LIMIT)


def _rms_mod(x, g, sh, sc):
    y = x * lax.rsqrt(jnp.mean(x * x, axis=-1, keepdims=True) + EPS)
    return (y * g) * (1.0 + sc) + sh


def _silu(x):
    return x * (1.0 / (1.0 + jnp.exp(-x)))


def _dot(a, b, **kw):
    return jnp.dot(a, b, preferred_element_type=F32, **kw)


def _mod_kernel(c_ref, w_ref, b_ref, o_ref):
    o_ref[...] = _dot(_silu(c_ref[...]), w_ref[...], precision=HI) + b_ref[...]


def _mod_call(cc, mod_w, mod_b):
    depth = mod_w.shape[0]
    tn = 1024
    return pl.pallas_call(
        _mod_kernel,
        grid=(depth, 6 * D // tn),
        in_specs=[
            pl.BlockSpec((8, D), lambda l, j: (0, 0)),
            pl.BlockSpec((None, D, tn), lambda l, j: (l, 0, j)),
            pl.BlockSpec((None, 1, tn), lambda l, j: (l, 0, j)),
        ],
        out_specs=pl.BlockSpec((None, 8, tn), lambda l, j: (l, 0, j)),
        out_shape=jax.ShapeDtypeStruct((depth, 8, 6 * D), F32),
        compiler_params=_params("parallel", "parallel"),
        name="mod",
    )(cc, mod_w, mod_b.reshape(depth, 1, 6 * D))


TM_IN = 1024
HALO = 8


def _dot_nt(a, b):
    return lax.dot_general(a, b, (((1,), (1,)), ((), ())), preferred_element_type=F32)


def _inproj_kernel(xp_ref, xm_ref, xn_ref, g_ref, sh_ref, sc_ref, why_ref, wqk_ref, wvt_ref, cos_ref, sin_ref,
                   cw_ref, cb_ref, hv_ref, hx1_ref, hx2_ref, q_ref, k_ref, vt_ref, buf_ref):
    i = pl.program_id(0)
    tiles = S // TM_IN
    first = (i % tiles) == 0
    last = (i % tiles) == tiles - 1
    g, sh, sc = g_ref[...], sh_ref[...], sc_ref[...]
    hn = _rms_mod(xm_ref[...], g, sh, sc).astype(BF16)
    hp = jnp.where(first, 0.0, _rms_mod(xp_ref[...], g, sh, sc)).astype(BF16)
    hx = jnp.where(last, 0.0, _rms_mod(xn_ref[...], g, sh, sc)).astype(BF16)
    why = why_ref[...]
    buf_ref[0:HALO, :] = _dot(hp, why)
    buf_ref[HALO:HALO + TM_IN, :] = _dot(hn, why)
    buf_ref[HALO + TM_IN:2 * HALO + TM_IN, :] = _dot(hx, why)
    cw = cw_ref[...]
    ext = buf_ref[...]
    prev = pltpu.roll(ext, 1, 0)[HALO:HALO + TM_IN]
    nxt = pltpu.roll(ext, TM_IN + 2 * HALO - 1, 0)[HALO:HALO + TM_IN]
    conv = cw[0:1] * prev + cw[1:2] * ext[HALO:HALO + TM_IN] + cw[2:3] * nxt + cb_ref[...]
    hv_ref[...] = conv[:, 0:DH]
    hx1_ref[...] = conv[:, DH:2 * DH]
    hx2_ref[...] = conv[:, 2 * DH:3 * DH]

    qkv = _dot(hn, wqk_ref[...])
    vt_ref[...] = _dot_nt(wvt_ref[...], hn).astype(BF16)
    cos, sin = cos_ref[...], sin_ref[...]
    lane = lax.broadcasted_iota(jnp.int32, (TM_IN, 128), 1)
    even = ((lane // 16) % 2) == 0

    def rope(xg):
        partner = jnp.where(even, pltpu.roll(xg, 112, 1), pltpu.roll(xg, 16, 1))
        return xg * cos + partner * sin

    scale = 64 ** -0.5 * math.log2(math.e)
    for h in range(HEADS):
        q_ref[:, h * 128:(h + 1) * 128] = (rope(qkv[:, h * 128:(h + 1) * 128]) * scale).astype(BF16)
        k_ref[:, h * 128:(h + 1) * 128] = rope(qkv[:, DD + h * 128:DD + (h + 1) * 128]).astype(BF16)


def _inproj_call(x2, g, sh, sc, why, wqk, wvt, cos_t, sin_t, cw, cb):
    tm = TM_IN
    tiles = S // tm
    nh = tm // HALO
    row = lambda i: (i, 0)
    bvec = pl.BlockSpec((None, 1, D), lambda i: (i // tiles, 0, 0))
    full = lambda shape: pl.BlockSpec(shape, lambda i: (0,) * len(shape))
    return pl.pallas_call(
        _inproj_kernel,
        grid=(T // tm,),
        in_specs=[
            pl.BlockSpec((HALO, D), lambda i: (jnp.maximum(i * nh - 1, 0), 0)),
            pl.BlockSpec((tm, D), row),
            pl.BlockSpec((HALO, D), lambda i: (jnp.minimum((i + 1) * nh, T // HALO - 1), 0)),
            full((1, D)), bvec, bvec,
            full((D, 3 * DH)), full((D, 2 * DD)), full((DD, D)),
            pl.BlockSpec((tm, 128), lambda i: (i % tiles, 0)),
            pl.BlockSpec((tm, 128), lambda i: (i % tiles, 0)),
            full((3, 3 * DH)), full((1, 3 * DH)),
        ],
        out_specs=[pl.BlockSpec((tm, DH), row)] * 3 + [pl.BlockSpec((tm, DD), row)] * 2
        + [pl.BlockSpec((None, DD, tm), lambda i: (i // tiles, 0, i % tiles))],
        out_shape=[jax.ShapeDtypeStruct((T, DH), F32)] * 3 + [jax.ShapeDtypeStruct((T, DD), BF16)] * 2
        + [jax.ShapeDtypeStruct((B, DD, S), BF16)],
        scratch_shapes=[pltpu.VMEM((tm + 2 * HALO, 3 * DH), F32)],
        compiler_params=_params("parallel"),
        name="inproj",
    )(x2, x2, x2, g, sh, sc, why, wqk, wvt, cos_t, sin_t, cw, cb)


def _ctxkv_kernel(x_ref, g_ref, sh_ref, sc_ref, wk_ref, wvt_ref, k_ref, vt_ref):
    hn = _rms_mod(x_ref[...], g_ref[...], sh_ref[...], sc_ref[...]).astype(BF16)
    k_ref[...] = _dot(hn, wk_ref[...]).astype(BF16)
    vt_ref[...] = _dot_nt(wvt_ref[...], hn).astype(BF16)


def _ctxkv_call(ctx2, g, sh, sc, wk, wvt):
    tm = CTX
    full = lambda shape: pl.BlockSpec(shape, lambda i: (0,) * len(shape))
    return pl.pallas_call(
        _ctxkv_kernel,
        grid=(B * CTX // tm,),
        in_specs=[pl.BlockSpec((tm, D), lambda i: (i, 0)), full((1, D)), full((1, D)), full((1, D)),
                  full((D, DD)), full((DD, D))],
        out_specs=[pl.BlockSpec((tm, DD), lambda i: (i, 0)), pl.BlockSpec((None, DD, tm), lambda i: (i, 0, 0))],
        out_shape=[jax.ShapeDtypeStruct((B * CTX, DD), BF16), jax.ShapeDtypeStruct((B, DD, CTX), BF16)],
        compiler_params=_params("parallel"),
        name="ctxkv",
    )(ctx2, g, sh, sc, wk, wvt)


TQ = 256
TK = 256
LAM_INIT0 = 0.8 - 0.6 * math.exp(-0.3 * 0)


def _attn_kernel(lam_ref, g_ref, q_ref, k_ref, vt_ref, kc_ref, vct_ref, o_ref, sa_ref, sb_ref, acc_ref):
    lane = lax.broadcasted_iota(jnp.int32, (TQ, 128), 1)
    lo = lane < 64
    lp = lam_ref[...]
    lam = (jnp.exp(jnp.sum(lp[0:1] * lp[1:2], axis=1, keepdims=True))
           - jnp.exp(jnp.sum(lp[2:3] * lp[3:4], axis=1, keepdims=True)) + LAM_INIT0)
    nblk = S // TK

    def key_block(j):
        if j == 0:
            return kc_ref[...], vct_ref[...], 0, CTX
        return (k_ref[(j - 1) * TK:j * TK, :], vt_ref[:, (j - 1) * TK:j * TK], CTX + (j - 1) * TK, TK)

    def load_q(t):
        q = q_ref[pl.ds(pl.multiple_of(t * TQ, TQ), TQ), :]
        zero = jnp.zeros_like(q)
        return jnp.where(lo, q, zero), jnp.where(lo, zero, q)

    def scores_block(qs, j, s_ref, m8):
        kblk, _, off, n = key_block(j)
        out = []
        for mi in range(2):
            s = _dot_nt(kblk, qs[mi])
            s_ref[mi, off:off + n, :] = s
            part = jnp.max(s.reshape(n // 8, 8, TQ), axis=0)
            out.append(part if m8 is None else jnp.maximum(m8[mi], part))
        return out

    def probs_block(j, s_ref, m, l8):
        _, vtblk, off, n = key_block(j)
        out = []
        for mi in range(2):
            p = jnp.exp2(s_ref[mi, off:off + n, :] - m[mi])
            part = jnp.sum(p.reshape(n // 8, 8, TQ), axis=0)
            out.append(part if l8 is None else l8[mi] + part)
            pv = _dot(vtblk, p.astype(BF16))
            if j == 0:
                acc_ref[mi] = pv
            else:
                acc_ref[mi] += pv
        return out

    def finish(t, l8):
        l = [jnp.sum(l8[mi], axis=0, keepdims=True) for mi in range(2)]
        o = acc_ref[0] / l[0] - lam * (acc_ref[1] / l[1])
        o = o * lax.rsqrt(jnp.mean(o * o, axis=0, keepdims=True) + EPS)
        o = (o * g_ref[...]) * (1.0 - LAM_INIT0)
        o_ref[pl.ds(pl.multiple_of(t * TQ, TQ), TQ), :] = o.T.astype(o_ref.dtype)

    def col_max(m8):
        return [jnp.max(m8[mi], axis=0, keepdims=True) for mi in range(2)]

    def stage(t_score, s_write, t_prob, s_read, m_read):
        qs = load_q(t_score)
        m8 = l8 = None
        for j in range(nblk + 1):
            m8 = scores_block(qs, j, s_write, m8)
            l8 = probs_block(j, s_read, m_read, l8)
        finish(t_prob, l8)
        return col_max(m8)

    qs0 = load_q(0)
    m8 = None
    for j in range(nblk + 1):
        m8 = scores_block(qs0, j, sa_ref, m8)
    m_a0 = col_max(m8)

    ntile = S // TQ
    assert ntile % 2 == 0

    def body(t, m):
        return lax.cond(t % 2 == 0,
                        lambda m: tuple(stage(t + 1, sb_ref, t, sa_ref, m)),
                        lambda m: tuple(stage(t + 1, sa_ref, t, sb_ref, m)), m)

    m_last = lax.fori_loop(0, ntile - 1, body, tuple(m_a0))
    l8 = None
    for j in range(nblk + 1):
        l8 = probs_block(j, sb_ref, m_last, l8)
    finish(ntile - 1, l8)


def _attn_call(lam_p, subln_g_col, q, k, vt, kc, vct):
    qspec = pl.BlockSpec((None, S, 128), lambda b, h: (b, 0, h))
    return pl.pallas_call(
        _attn_kernel,
        grid=(B, HEADS),
        in_specs=[pl.BlockSpec((4, 64), lambda b, h: (0, 0)), pl.BlockSpec((128, 1), lambda b, h: (0, 0)),
                  qspec, qspec,
                  pl.BlockSpec((None, 128, S), lambda b, h: (b, h, 0)),
                  pl.BlockSpec((None, CTX, 128), lambda b, h: (b, 0, h)),
                  pl.BlockSpec((None, 128, CTX), lambda b, h: (b, h, 0))],
        out_specs=qspec,
        out_shape=jax.ShapeDtypeStruct((B, S, DD), BF16),
        scratch_shapes=[pltpu.VMEM((2, CTX + S, TQ), F32), pltpu.VMEM((2, CTX + S, TQ), F32),
                        pltpu.VMEM((2, 128, TQ), F32)],
        compiler_params=_params("parallel", "parallel"),
        name="diffattn",
    )(lam_p, subln_g_col, q, k, vt, kc, vct)


def _cplx_block(m):
    return np.block([[m.real, -m.imag], [m.imag, m.real]])


@functools.lru_cache(maxsize=None)
def _dft_tables():
    k = np.arange(R)
    w_r = np.exp(-2j * np.pi * np.outer(k, k) / R)
    f_a = _cplx_block(w_r[:, :HALF])
    f_a_real = np.concatenate([w_r.real, w_r.imag], axis=0)
    g_a = _cplx_block(np.conj(w_r)[:HALF, :])
    tw = np.exp(-2j * np.pi * np.outer(k, k) / NFFT)
    f32 = lambda a: np.ascontiguousarray(a, dtype=np.float32)
    return (f32(f_a), f32(f_a_real), f32(g_a), f32(w_r.real), f32(w_r.imag), f32(tw.real), f32(tw.imag))


@functools.lru_cache(maxsize=None)
def _filter_positions():
    idx = np.arange(NFFT)
    pos = np.where(idx < S, idx, NFFT - idx) % S
    bands = (FILTER_EMB - 1) // 2
    t = np.linspace(0.0, 1.0, S)[:, None]
    ang = (2.0 * math.pi / S) * np.arange(S)[:, None] * np.linspace(1e-4, bands - 1, bands)[None, :]
    z = np.concatenate([t, np.cos(ang), -np.sin(ang)], axis=-1)
    z = np.pad(z, ((0, 0), (0, FILTER_EMB_PAD - FILTER_EMB)))
    return np.ascontiguousarray(z[pos], dtype=np.float32)


def _decay_rates():
    return jnp.abs(jnp.linspace(math.log(1e-2) / 1.5, math.log(1e-2) / 0.3, DH, dtype=F32))[None, :]


@functools.lru_cache(maxsize=None)
def _rope_tables():
    rows = S // GRID_W
    row = np.repeat(np.arange(rows), GRID_W).astype(np.float64)
    col = np.tile(np.arange(GRID_W), rows).astype(np.float64)
    m = 16
    inv = 10000.0 ** (-np.arange(m) / m)
    ang_r = row[:, None] * inv[None, :]
    ang_c = col[:, None] * inv[None, :]
    cos = np.concatenate([np.cos(ang_r)] * 2 + [np.cos(ang_c)] * 2, axis=1)
    sin = np.concatenate([-np.sin(ang_r), np.sin(ang_r), -np.sin(ang_c), np.sin(ang_c)], axis=1)
    f32 = lambda a: np.ascontiguousarray(np.concatenate([a, a], axis=1), dtype=np.float32)
    return f32(cos), f32(sin)


TR_F = 512


def _filter_kernel(zt_ref, t_ref, w1t_ref, b1_ref, fr_ref, w2t_ref, b2_ref, w3_ref, dec_ref, o_ref):
    i = pl.program_id(0)
    fr = fr_ref[...]
    h = jnp.sin(fr[:, 0:1] * (_dot3(w1t_ref[...], zt_ref[...]) + b1_ref[...]))
    h = jnp.sin(fr[:, 1:2] * (_dot3(w2t_ref[...], h) + b2_ref[...]))
    h = h.T
    window = jnp.exp(-t_ref[...] * dec_ref[...]) + 0.05
    ridx = i * TR_F + lax.broadcasted_iota(jnp.int32, (TR_F, 1), 0)
    for o in range(2):
        o_ref[o] = jnp.where(ridx == S, 0.0, _dot3(h, w3_ref[o]) * window)


def _filter_call(zt, tcol, w1t, b1c, freqc, w2t, b2c, w3r, dec):
    full = lambda shape: pl.BlockSpec(shape, lambda i: (0,) * len(shape))
    half_tiles = S // TR_F
    return pl.pallas_call(
        _filter_kernel,
        grid=(NFFT // TR_F,),
        in_specs=[pl.BlockSpec((FILTER_EMB_PAD, TR_F), lambda i: (0, i)), pl.BlockSpec((TR_F, 1), lambda i: (i, 0)),
                  full((FILTER_WIDTH, FILTER_EMB_PAD)), full((FILTER_WIDTH, 1)), full((FILTER_WIDTH, 2)),
                  full((FILTER_WIDTH, FILTER_WIDTH)), full((FILTER_WIDTH, 1)),
                  pl.BlockSpec((2, None, FILTER_WIDTH, DH), lambda i: (0, i // half_tiles, 0, 0)),
                  full((1, DH))],
        out_specs=pl.BlockSpec((2, TR_F, DH), lambda i: (0, i, 0)),
        out_shape=jax.ShapeDtypeStruct((2, NFFT, DH), F32),
        compiler_params=_params("parallel"),
        name="hyena_filter",
    )(zt, tcol, w1t, b1c, freqc, w2t, b2c, w3r, dec)


NB = 8


def _split_bf16(a):
    hi = a.astype(BF16)
    return hi, (a - hi.astype(F32)).astype(BF16)


def _dot3(a, b):
    a_hi, a_lo = _split_bf16(a)
    b_hi, b_lo = _split_bf16(b)
    return _dot(a_hi, b_hi) + (_dot(a_hi, b_lo) + _dot(a_lo, b_hi))


def _fspec_a_kernel(f_ref, h_ref, o_ref, x_ref):
    f_hi, f_lo = _split_bf16(f_ref[...])
    for j in range(NB):
        x_ref[j % 2] = h_ref[:, j, :]
        x_hi, x_lo = _split_bf16(x_ref[j % 2])
        o_ref[j] = _dot(f_hi, x_hi) + (_dot(f_hi, x_lo) + _dot(f_lo, x_hi))


def _fspec_a_call(f_a_real, hview):
    return pl.pallas_call(
        _fspec_a_kernel,
        grid=(2, R // NB),
        in_specs=[pl.BlockSpec((2 * R, R), lambda o, j: (0, 0)),
                  pl.BlockSpec((None, R, NB, DH), lambda o, j: (o, 0, j, 0))],
        out_specs=pl.BlockSpec((None, NB, 2 * R, DH), lambda o, j: (o, j, 0, 0)),
        out_shape=jax.ShapeDtypeStruct((2, R, 2 * R, DH), F32),
        scratch_shapes=[pltpu.VMEM((2, R, DH), F32)],
        compiler_params=_params("parallel", "parallel"),
        name="hyena_fspec_a",
    )(f_a_real, hview)


def _rows2(ref, j):
    return jnp.concatenate([ref[:, 0, j, :], ref[:, 1, j, :]], axis=0)


def _conv_a_kernel(f_ref, z_ref, o_ref):
    f = f_ref[...]
    for j in range(NB):
        zz = jnp.concatenate([z_ref[0, :, j, :], z_ref[1, :, j, :]], axis=0).astype(BF16)
        o_ref[j] = _dot(f, zz)


def _conv_a_call(f_a, zview):
    return pl.pallas_call(
        _conv_a_kernel,
        grid=(2, R // NB),
        in_specs=[pl.BlockSpec((2 * R, R), lambda p, j: (0, 0)),
                  pl.BlockSpec((2, None, HALF, NB, DH), lambda p, j: (0, p, 0, j, 0))],
        out_specs=pl.BlockSpec((None, NB, 2 * R, DH), lambda p, j: (p, j, 0, 0)),
        out_shape=jax.ShapeDtypeStruct((2, R, 2 * R, DH), F32),
        compiler_params=_params("parallel", "parallel"),
        name="hyena_conv_a",
    )(f_a, zview)


def _conv_c_kernel(wr_ref, wi_ref, tr_ref, ti_ref, af_ref, a_ref, o_ref, x_ref):
    wr, wi = wr_ref[...], wi_ref[...]
    for kk in range(NB):
        tr, ti = tr_ref[kk:kk + 1, :], ti_ref[kk:kk + 1, :]
        mr = wr * tr - wi * ti
        mi = wr * ti + wi * tr
        mb = jnp.concatenate([jnp.concatenate([mr, -mi], axis=1), jnp.concatenate([mi, mr], axis=1)], axis=0)
        m_hi, m_lo = _split_bf16(mb)
        x_ref[kk % 2] = _rows2(af_ref, kk)
        f_hi, f_lo = _split_bf16(x_ref[kk % 2])
        h = (_dot(m_hi, f_hi) + (_dot(m_hi, f_lo) + _dot(m_lo, f_hi))) * (1.0 / NFFT)
        hr, hi = h[0:R], h[R:2 * R]
        a = jnp.concatenate([_rows2(a_ref.at[0], kk), _rows2(a_ref.at[1], kk)], axis=1).astype(BF16)
        x = _dot(m_hi, a)
        ys = []
        for p in range(2):
            xr, xi = x[0:R, p * DH:(p + 1) * DH], x[R:2 * R, p * DH:(p + 1) * DH]
            ys.append(jnp.concatenate([xr * hr - xi * hi, xr * hi + xi * hr], axis=0))
        b = _dot(mb.T.astype(BF16), jnp.concatenate(ys, axis=1).astype(BF16))
        o_ref[0, kk] = b[:, 0:DH]
        o_ref[1, kk] = b[:, DH:2 * DH]


def _conv_c_call(w_re, w_im, tw_re, tw_im, af_view, order, a_view):
    full = pl.BlockSpec((R, R), lambda k: (0, 0))
    trow = pl.BlockSpec((NB, R), lambda k: (k, 0))
    return pl.pallas_call(
        _conv_c_kernel,
        grid=(R // NB,),
        in_specs=[full, full, trow, trow,
                  pl.BlockSpec((None, R, 2, NB, DH), lambda k: (order, 0, 0, k, 0)),
                  pl.BlockSpec((2, R, 2, NB, DH), lambda k: (0, 0, 0, k, 0))],
        out_specs=pl.BlockSpec((2, NB, 2 * R, DH), lambda k: (0, k, 0, 0)),
        out_shape=jax.ShapeDtypeStruct((2, R, 2 * R, DH), F32),
        scratch_shapes=[pltpu.VMEM((2, 2 * R, DH), F32)],
        compiler_params=_params("parallel"),
        name="hyena_conv_c",
    )(w_re, w_im, tw_re, tw_im, af_view, a_view)


def _conv_out_kernel(first_order, g_ref, f_ref, b_ref, z_ref, gate_ref, d_ref, zo_ref, *next_a):
    g, d = g_ref[...], d_ref[...]
    for j in range(NB):
        y = _dot(g, _rows2(b_ref, j).astype(BF16))
        zn = []
        for bi in range(2):
            zc = z_ref[bi, :, j, :] if first_order else z_ref[bi, j]
            zn.append(gate_ref[bi, :, j, :] * (y[bi * HALF:(bi + 1) * HALF] + d * zc))
            zo_ref[bi, j] = zn[bi]
        if first_order:
            next_a[0][j] = _dot(f_ref[...], jnp.concatenate(zn, axis=0).astype(BF16))


def _conv_out_call(g_a, f_a, b_view, z_in, gate_view, drow, first_order):
    pm = pl.BlockSpec((2, None, HALF, NB, DH), lambda p, j: (0, p, 0, j, 0))
    nm = pl.BlockSpec((2, None, NB, HALF, DH), lambda p, j: (0, p, j, 0, 0))
    out_specs = [nm]
    out_shape = [jax.ShapeDtypeStruct((2, 2, R, HALF, DH), F32)]
    if first_order:
        out_specs.append(pl.BlockSpec((None, NB, 2 * R, DH), lambda p, j: (p, j, 0, 0)))
        out_shape.append(jax.ShapeDtypeStruct((2, R, 2 * R, DH), F32))
    return pl.pallas_call(
        functools.partial(_conv_out_kernel, first_order),
        grid=(2, R // NB),
        in_specs=[pl.BlockSpec((R, 2 * R), lambda p, j: (0, 0)), pl.BlockSpec((2 * R, R), lambda p, j: (0, 0)),
                  pl.BlockSpec((None, R, 2, NB, DH), lambda p, j: (p, 0, 0, j, 0)),
                  pm if first_order else nm, pm, pl.BlockSpec((1, DH), lambda p, j: (0, 0))],
        out_specs=out_specs,
        out_shape=out_shape,
        compiler_params=_params("parallel", "parallel"),
        name="hyena_conv_out0" if first_order else "hyena_conv_out1",
    )(g_a, f_a, b_view, z_in, gate_view, drow)


TM_OUT = NB * R


def _outproj_kernel(x_ref, ya_ref, yb_ref, wa_ref, wb_ref, g1_ref, o_ref):
    ya = jnp.concatenate([ya_ref[:, r, :] for r in range(NB)], axis=0).astype(BF16)
    mix = _dot(ya, wa_ref[...]) + _dot(yb_ref[...], wb_ref[...])
    o_ref[...] = x_ref[...] + g1_ref[...] * mix


def _outproj_call(x2, ya_nm, yb, wa, wb, g1):
    tm = TM_OUT
    tiles = S // tm
    row = lambda i: (i, 0)
    return pl.pallas_call(
        _outproj_kernel,
        grid=(T // tm,),
        in_specs=[pl.BlockSpec((tm, D), row),
                  pl.BlockSpec((None, R, NB, DH), lambda i: (i // tiles, 0, i % tiles, 0)),
                  pl.BlockSpec((tm, DD), row),
                  pl.BlockSpec((DH, D), lambda i: (0, 0)), pl.BlockSpec((DD, D), lambda i: (0, 0)),
                  pl.BlockSpec((None, 1, D), lambda i: (i // tiles, 0, 0))],
        out_specs=pl.BlockSpec((tm, D), row),
        out_shape=jax.ShapeDtypeStruct((T, D), F32),
        compiler_params=_params("parallel"),
        name="outproj",
    )(x2, ya_nm, yb, wa, wb, g1)


TM_FFN = 1024
FH = 16
CK = 256
assert D_FF % CK == 0


def _ffn_kernel(final, xp_ref, xm_ref, xn_ref, g_ref, sh_ref, sc_ref, g2_ref, wup_ref, cw_ref, cb_ref, wdn_ref,
                fg_ref, o_ref, hn_ref, act_ref):
    i = pl.program_id(0)
    tiles = S // TM_FFN
    first = (i % tiles) == 0
    last = (i % tiles) == tiles - 1
    g, sh, sc = g_ref[...], sh_ref[...], sc_ref[...]
    xm = xm_ref[...]
    hn_ref[0:FH, :] = jnp.where(first, 0.0, _rms_mod(xp_ref[...], g, sh, sc)).astype(BF16)
    hn_ref[FH:FH + TM_FFN, :] = _rms_mod(xm, g, sh, sc).astype(BF16)
    hn_ref[FH + TM_FFN:2 * FH + TM_FFN, :] = jnp.where(last, 0.0, _rms_mod(xn_ref[...], g, sh, sc)).astype(BF16)
    hn = hn_ref[...]

    rows = TM_FFN + 2 * FH

    def conv(u, c0):
        w = cw_ref[:, c0:c0 + CK]
        prev = pltpu.roll(u, 1, 0)[FH:FH + TM_FFN]
        nxt = pltpu.roll(u, rows - 1, 0)[FH:FH + TM_FFN]
        return w[0:1] * prev + w[1:2] * u[FH:FH + TM_FFN] + w[2:3] * nxt + cb_ref[:, c0:c0 + CK]

    for c in range(D_FF // CK):
        c0 = c * CK
        ug = _dot(hn, wup_ref[:, c0:c0 + CK])
        uu = _dot(hn, wup_ref[:, D_FF + c0:D_FF + c0 + CK])
        act_ref[:, c0:c0 + CK] = (_silu(conv(ug, c0)) * conv(uu, D_FF + c0)).astype(BF16)
    out = xm + g2_ref[...] * _dot(act_ref[...], wdn_ref[...])
    if final:
        out = (out * lax.rsqrt(jnp.mean(out * out, axis=-1, keepdims=True) + EPS)) * fg_ref[...]
    o_ref[...] = out


def _ffn_call(x2, g, sh, sc, g2, wup, cw, cb, wdn, fg, final):
    tm = TM_FFN
    tiles = S // tm
    nh = tm // FH
    row = lambda i: (i, 0)
    bvec = pl.BlockSpec((None, 1, D), lambda i: (i // tiles, 0, 0))
    full = lambda shape: pl.BlockSpec(shape, lambda i: (0,) * len(shape))
    once = lambda shape: pl.BlockSpec(shape, lambda i: (0,) * len(shape), pipeline_mode=pl.Buffered(1))
    return pl.pallas_call(
        functools.partial(_ffn_kernel, final),
        grid=(T // tm,),
        in_specs=[
            pl.BlockSpec((FH, D), lambda i: (jnp.maximum(i * nh - 1, 0), 0)),
            pl.BlockSpec((tm, D), row),
            pl.BlockSpec((FH, D), lambda i: (jnp.minimum((i + 1) * nh, T // FH - 1), 0)),
            full((1, D)), bvec, bvec, bvec,
            once((D, 2 * D_FF)), full((3, 2 * D_FF)), full((1, 2 * D_FF)), once((D_FF, D)), full((1, D)),
        ],
        out_specs=pl.BlockSpec((tm, D), row),
        out_shape=jax.ShapeDtypeStruct((T, D), F32),
        scratch_shapes=[pltpu.VMEM((tm + 2 * FH, D), BF16), pltpu.VMEM((tm, D_FF), BF16)],
        compiler_params=_params("parallel"),
        name="convffn_final" if final else "convffn",
    )(x2, x2, x2, g, sh, sc, g2, wup, cw, cb, wdn, fg)


TM_SGU = 1024
SUB_SGU = 512


def _sgu_kernel(x_ref, g_ref, sh_ref, sc_ref, g1_ref, win_ref, bin_ref, lng_ref, lnb_ref, ws_ref, bs_ref, wout_ref,
                o_ref, s_ref):
    cg = D // GROUPS
    for r0 in range(0, TM_SGU, SUB_SGU):
        x = x_ref[r0:r0 + SUB_SGU, :]
        hn = _rms_mod(x, g_ref[...], sh_ref[...], sc_ref[...]).astype(BF16)
        pre = _dot(hn, win_ref[...]) + bin_ref[...]
        act = 0.5 * pre * (1.0 + lax.erf(pre * (2.0 ** -0.5)))
        u = act[:, 0:D]
        v = act[:, D:2 * D]
        mu = jnp.mean(v, axis=-1, keepdims=True)
        vc = v - mu
        v = (vc * lax.rsqrt(jnp.mean(vc * vc, axis=-1, keepdims=True) + EPS)) * lng_ref[...] + lnb_ref[...]
        vb = v.astype(BF16)
        for n in range(SUB_SGU // CHUNK):
            for gi in range(GROUPS):
                s_ref[r0 + n * CHUNK:r0 + (n + 1) * CHUNK, gi * cg:(gi + 1) * cg] = (
                    _dot(ws_ref[gi], vb[n * CHUNK:(n + 1) * CHUNK, gi * cg:(gi + 1) * cg])
                    + bs_ref[:, gi * cg:(gi + 1) * cg])
        gated = (u * s_ref[r0:r0 + SUB_SGU, :]).astype(BF16)
        o_ref[r0:r0 + SUB_SGU, :] = x + g1_ref[...] * _dot(gated, wout_ref[...])


def _sgu_call(x2, g, sh, sc, g1, win, bin_, lng, lnb, ws, bs_full, wout):
    tm = TM_SGU
    tiles = S // tm
    row = lambda i: (i, 0)
    bvec = pl.BlockSpec((None, 1, D), lambda i: (i // tiles, 0, 0))
    full = lambda shape: pl.BlockSpec(shape, lambda i: (0,) * len(shape))
    return pl.pallas_call(
        _sgu_kernel,
        grid=(T // tm,),
        in_specs=[pl.BlockSpec((tm, D), row), full((1, D)), bvec, bvec, bvec,
                  full((D, 2 * D)), full((1, 2 * D)), full((1, D)), full((1, D)),
                  full((GROUPS, CHUNK, CHUNK)), full((CHUNK, D)), full((D, D))],
        out_specs=pl.BlockSpec((tm, D), row),
        out_shape=jax.ShapeDtypeStruct((T, D), F32),
        scratch_shapes=[pltpu.VMEM((tm, D), F32)],
        compiler_params=_params("parallel"),
        name="sgu",
    )(x2, g, sh, sc, g1, win, bin_, lng, lnb, ws, bs_full, wout)


def kernel(x, c, ctx, c_ctx, mod_w, mod_b, norm_mix_g, norm_ffn_g, ffn_w_up, ffn_conv_w, ffn_conv_b, ffn_w_down,
           ab_w_in, hy_conv_w, hy_conv_b, hy_w1, hy_b1, hy_freq, hy_w2, hy_b2, hy_w3, hy_bias, da_lambda,
           da_subln_g, ab_w_out, sgu_w_in, sgu_b_in, sgu_ln_g, sgu_ln_b, sgu_w_s, sgu_b_s, sgu_w_out,
           final_norm_g):
    x2 = x.reshape(T, D)
    cc = jnp.concatenate([c, c_ctx[None, :], jnp.zeros((3, D), F32)], axis=0)
    mods = _mod_call(cc, mod_w, mod_b)

    def mod_vec(layer, k):
        return mods[layer, 0:B, k * D:(k + 1) * D].reshape(B, 1, D)

    w_in = ab_w_in[0].astype(BF16)
    n_hq = 3 * DH + DD
    cos_t, sin_t = (jnp.asarray(t) for t in _rope_tables())
    w_k = w_in[:, n_hq:n_hq + DD]
    w_vt = w_in[:, n_hq + DD:].T
    hv, hx1, hx2, q, k, vt = _inproj_call(
        x2, norm_mix_g[0][None, :], mod_vec(0, 0), mod_vec(0, 1),
        w_in[:, 0:3 * DH], jnp.concatenate([w_in[:, 3 * DH:n_hq], w_k], axis=1), w_vt,
        cos_t, sin_t, hy_conv_w[0], hy_conv_b[0][None, :])
    kc, vct = _ctxkv_call(ctx.reshape(B * CTX, D), norm_mix_g[0][None, :], mods[0, 4:5, 0:D], mods[0, 4:5, D:2 * D],
                          w_k, w_vt)
    yb = _attn_call(da_lambda[0], da_subln_g[0][:, None], q.reshape(B, S, DD), k.reshape(B, S, DD), vt,
                    kc.reshape(B, CTX, DD), vct)

    f_a, f_a_real, g_a, w_re, w_im, tw_re, tw_im = (jnp.asarray(t) for t in _dft_tables())
    w1p = jnp.pad(hy_w1[0], ((0, FILTER_EMB_PAD - FILTER_EMB), (0, 0)))
    w3r = hy_w3[0].reshape(FILTER_WIDTH, 2, 2, DH).transpose(1, 2, 0, 3)
    zpos = _filter_positions()
    filt = _filter_call(jnp.asarray(np.ascontiguousarray(zpos.T)), jnp.asarray(zpos[:, 0:1]), w1p.T,
                        hy_b1[0][:, None], hy_freq[0].T, hy_w2[0].T, hy_b2[0][:, None], w3r, _decay_rates())
    af_view = _fspec_a_call(f_a_real, filt.reshape(2, R, R, DH)).reshape(2, R, 2, R, DH)

    f_a_b, g_a_b = f_a.astype(BF16), g_a.astype(BF16)
    pm = lambda t: t.reshape(2, 2, HALF, R, DH)
    a = _conv_a_call(f_a_b, pm(hv))
    bq = _conv_c_call(w_re, w_im, tw_re, tw_im, af_view, 0, a.reshape(2, R, 2, R, DH))
    z1, a = _conv_out_call(g_a_b, f_a_b, bq.reshape(2, R, 2, R, DH), pm(hv), pm(hx1), hy_bias[0, 0][None, :], True)
    bq = _conv_c_call(w_re, w_im, tw_re, tw_im, af_view, 1, a.reshape(2, R, 2, R, DH))
    (z2,) = _conv_out_call(g_a_b, f_a_b, bq.reshape(2, R, 2, R, DH), z1, pm(hx2), hy_bias[0, 1][None, :], False)

    w_out = ab_w_out[0].astype(BF16)
    x2 = _outproj_call(x2, z2.reshape(B, R, HALF, DH), yb.reshape(T, DD), w_out[0:DH], w_out[DH:], mod_vec(0, 2))
    x2 = _ffn_call(x2, norm_ffn_g[0][None, :], mod_vec(0, 3), mod_vec(0, 4), mod_vec(0, 5),
                   ffn_w_up[0].astype(BF16), ffn_conv_w[0], ffn_conv_b[0][None, :], ffn_w_down[0].astype(BF16),
                   final_norm_g[None, :], False)

    bs_full = jnp.repeat(sgu_b_s[0].T, D // GROUPS, axis=1)
    x2 = _sgu_call(x2, norm_mix_g[1][None, :], mod_vec(1, 0), mod_vec(1, 1), mod_vec(1, 2),
                   sgu_w_in[0].astype(BF16), sgu_b_in[0][None, :], sgu_ln_g[0][None, :], sgu_ln_b[0][None, :],
                   sgu_w_s[0].astype(BF16), bs_full, sgu_w_out[0].astype(BF16))
    x2 = _ffn_call(x2, norm_ffn_g[1][None, :], mod_vec(1, 3), mod_vec(1, 4), mod_vec(1, 5),
                   ffn_w_up[1].astype(BF16), ffn_conv_w[1], ffn_conv_b[1][None, :], ffn_w_down[1].astype(BF16),
                   final_norm_g[None, :], True)
    return x2.reshape(B, S, D)
```

```python
import functools
import math

import numpy as np
import jax
import jax.numpy as jnp
from jax import lax
from jax.experimental import pallas as pl
from jax.experimental.pallas import tpu as pltpu

D = 1024
B = 4
S = 8192
T = B * S
GRID_W = 64
CTX = 256
EPS = 1e-6
DH = 512
DD = 512
HEADS = 4
FILTER_EMB = 33
FILTER_EMB_PAD = 64
FILTER_WIDTH = 64
D_FF = 2816
CHUNK = 128
GROUPS = 8

NFFT = 2 * S
R = 128
HALF = R // 2
WCOL = R * DH

F32 = jnp.float32
BF16 = jnp.bfloat16
HI = lax.Precision.HIGHEST

VMEM_LIMIT = 56 * 1024 * 1024


def _params(*sem):
    return pltpu.CompilerParams(dimension_semantics=sem, vmem_limit_bytes=VMEM_LIMIT)


def _rms_mod(x, g, sh, sc):
    y = x * lax.rsqrt(jnp.mean(x * x, axis=-1, keepdims=True) + EPS)
    return (y * g) * (1.0 + sc) + sh


def _silu(x):
    h = 0.5 * x
    return h + h * jnp.tanh(h)


def _dot(a, b, **kw):
    return jnp.dot(a, b, preferred_element_type=F32, **kw)


def _mod_kernel(c_ref, w_ref, b_ref, o_ref):
    o_ref[...] = _dot(_silu(c_ref[...]), w_ref[...], precision=HI) + b_ref[...]


def _mod_call(cc, mod_w, mod_b):
    depth = mod_w.shape[0]
    tn = 1024
    return pl.pallas_call(
        _mod_kernel,
        grid=(depth, 6 * D // tn),
        in_specs=[
            pl.BlockSpec((8, D), lambda l, j: (0, 0)),
            pl.BlockSpec((None, D, tn), lambda l, j: (l, 0, j)),
            pl.BlockSpec((None, 1, tn), lambda l, j: (l, 0, j)),
        ],
        out_specs=pl.BlockSpec((None, 8, tn), lambda l, j: (l, 0, j)),
        out_shape=jax.ShapeDtypeStruct((depth, 8, 6 * D), F32),
        compiler_params=_params("parallel", "parallel"),
        name="mod",
    )(cc, mod_w, mod_b.reshape(depth, 1, 6 * D))


TM_IN = 1024
HALO = 8


def _dot_nt(a, b):
    return lax.dot_general(a, b, (((1,), (1,)), ((), ())), preferred_element_type=F32)


def _inproj_kernel(xp_ref, xm_ref, xn_ref, g_ref, sh_ref, sc_ref, why_ref, wqk_ref, wvt_ref, cos_ref, sin_ref,
                   cw_ref, cb_ref, hv_ref, hx1_ref, hx2_ref, q_ref, k_ref, vt_ref, buf_ref):
    i = pl.program_id(0)
    tiles = S // TM_IN
    first = (i % tiles) == 0
    last = (i % tiles) == tiles - 1
    g, sh, sc = g_ref[...], sh_ref[...], sc_ref[...]
    hn = _rms_mod(xm_ref[...], g, sh, sc).astype(BF16)
    hp = jnp.where(first, 0.0, _rms_mod(xp_ref[...], g, sh, sc)).astype(BF16)
    hx = jnp.where(last, 0.0, _rms_mod(xn_ref[...], g, sh, sc)).astype(BF16)
    why = why_ref[...]
    buf_ref[0:HALO, :] = _dot(hp, why)
    buf_ref[HALO:HALO + TM_IN, :] = _dot(hn, why)
    buf_ref[HALO + TM_IN:2 * HALO + TM_IN, :] = _dot(hx, why)
    cw = cw_ref[...]
    ext = buf_ref[...]
    prev = pltpu.roll(ext, 1, 0)[HALO:HALO + TM_IN]
    nxt = pltpu.roll(ext, TM_IN + 2 * HALO - 1, 0)[HALO:HALO + TM_IN]
    conv = cw[0:1] * prev + cw[1:2] * ext[HALO:HALO + TM_IN] + cw[2:3] * nxt + cb_ref[...]
    hv_ref[...] = conv[:, 0:DH]
    hx1_ref[...] = conv[:, DH:2 * DH]
    hx2_ref[...] = conv[:, 2 * DH:3 * DH]

    qkv = _dot(hn, wqk_ref[...])
    vt_ref[...] = _dot_nt(wvt_ref[...], hn).astype(BF16)
    cos, sin = cos_ref[...], sin_ref[...]
    lane = lax.broadcasted_iota(jnp.int32, (TM_IN, 128), 1)
    even = ((lane // 16) % 2) == 0

    def rope(xg):
        partner = jnp.where(even, pltpu.roll(xg, 112, 1), pltpu.roll(xg, 16, 1))
        return xg * cos + partner * sin

    scale = 64 ** -0.5 * math.log2(math.e)
    for h in range(HEADS):
        q_ref[:, h * 128:(h + 1) * 128] = (rope(qkv[:, h * 128:(h + 1) * 128]) * scale).astype(BF16)
        k_ref[:, h * 128:(h + 1) * 128] = rope(qkv[:, DD + h * 128:DD + (h + 1) * 128]).astype(BF16)


def _inproj_call(x2, g, sh, sc, why, wqk, wvt, cos_t, sin_t, cw, cb):
    tm = TM_IN
    tiles = S // tm
    nh = tm // HALO
    row = lambda i: (i, 0)
    bvec = pl.BlockSpec((None, 1, D), lambda i: (i // tiles, 0, 0))
    full = lambda shape: pl.BlockSpec(shape, lambda i: (0,) * len(shape))
    return pl.pallas_call(
        _inproj_kernel,
        grid=(T // tm,),
        in_specs=[
            pl.BlockSpec((HALO, D), lambda i: (jnp.maximum(i * nh - 1, 0), 0)),
            pl.BlockSpec((tm, D), row),
            pl.BlockSpec((HALO, D), lambda i: (jnp.minimum((i + 1) * nh, T // HALO - 1), 0)),
            full((1, D)), bvec, bvec,
            full((D, 3 * DH)), full((D, 2 * DD)), full((DD, D)),
            pl.BlockSpec((tm, 128), lambda i: (i % tiles, 0)),
            pl.BlockSpec((tm, 128), lambda i: (i % tiles, 0)),
            full((3, 3 * DH)), full((1, 3 * DH)),
        ],
        out_specs=[pl.BlockSpec((tm, DH), row)] * 3 + [pl.BlockSpec((tm, DD), row)] * 2
        + [pl.BlockSpec((None, DD, tm), lambda i: (i // tiles, 0, i % tiles))],
        out_shape=[jax.ShapeDtypeStruct((T, DH), F32)] * 3 + [jax.ShapeDtypeStruct((T, DD), BF16)] * 2
        + [jax.ShapeDtypeStruct((B, DD, S), BF16)],
        scratch_shapes=[pltpu.VMEM((tm + 2 * HALO, 3 * DH), F32)],
        compiler_params=_params("parallel"),
        name="inproj",
    )(x2, x2, x2, g, sh, sc, why, wqk, wvt, cos_t, sin_t, cw, cb)


def _ctxkv_kernel(x_ref, g_ref, sh_ref, sc_ref, wk_ref, wvt_ref, k_ref, vt_ref):
    hn = _rms_mod(x_ref[...], g_ref[...], sh_ref[...], sc_ref[...]).astype(BF16)
    k_ref[...] = _dot(hn, wk_ref[...]).astype(BF16)
    vt_ref[...] = _dot_nt(wvt_ref[...], hn).astype(BF16)


def _ctxkv_call(ctx2, g, sh, sc, wk, wvt):
    tm = CTX
    full = lambda shape: pl.BlockSpec(shape, lambda i: (0,) * len(shape))
    return pl.pallas_call(
        _ctxkv_kernel,
        grid=(B * CTX // tm,),
        in_specs=[pl.BlockSpec((tm, D), lambda i: (i, 0)), full((1, D)), full((1, D)), full((1, D)),
                  full((D, DD)), full((DD, D))],
        out_specs=[pl.BlockSpec((tm, DD), lambda i: (i, 0)), pl.BlockSpec((None, DD, tm), lambda i: (i, 0, 0))],
        out_shape=[jax.ShapeDtypeStruct((B * CTX, DD), BF16), jax.ShapeDtypeStruct((B, DD, CTX), BF16)],
        compiler_params=_params("parallel"),
        name="ctxkv",
    )(ctx2, g, sh, sc, wk, wvt)


TQ = 256
TK = 256
LAM_INIT0 = 0.8 - 0.6 * math.exp(-0.3 * 0)


def _attn_kernel(lam_ref, g_ref, q_ref, k_ref, vt_ref, kc_ref, vct_ref, o_ref, sa_ref, sb_ref, acc_ref):
    lane = lax.broadcasted_iota(jnp.int32, (TQ, 128), 1)
    lo = lane < 64
    lp = lam_ref[...]
    lam = (jnp.exp(jnp.sum(lp[0:1] * lp[1:2], axis=1, keepdims=True))
           - jnp.exp(jnp.sum(lp[2:3] * lp[3:4], axis=1, keepdims=True)) + LAM_INIT0)
    nblk = S // TK

    def key_block(j):
        if j == 0:
            return kc_ref[...], vct_ref[...], 0, CTX
        return (k_ref[(j - 1) * TK:j * TK, :], vt_ref[:, (j - 1) * TK:j * TK], CTX + (j - 1) * TK, TK)

    def load_q(t):
        q = q_ref[pl.ds(pl.multiple_of(t * TQ, TQ), TQ), :]
        zero = jnp.zeros_like(q)
        return jnp.where(lo, q, zero), jnp.where(lo, zero, q)

    def scores_block(qs, j, s_ref, m8):
        kblk, _, off, n = key_block(j)
        out = []
        for mi in range(2):
            s = _dot_nt(kblk, qs[mi])
            s_ref[mi, off:off + n, :] = s
            part = jnp.max(s.reshape(n // 8, 8, TQ), axis=0)
            out.append(part if m8 is None else jnp.maximum(m8[mi], part))
        return out

    def probs_block(j, s_ref, m, l8):
        _, vtblk, off, n = key_block(j)
        out = []
        for mi in range(2):
            p = jnp.exp2(s_ref[mi, off:off + n, :] - m[mi])
            part = jnp.sum(p.reshape(n // 8, 8, TQ), axis=0)
            out.append(part if l8 is None else l8[mi] + part)
            pv = _dot(vtblk, p.astype(BF16))
            if j == 0:
                acc_ref[mi] = pv
            else:
                acc_ref[mi] += pv
        return out

    def finish(t, l8):
        l = [jnp.sum(l8[mi], axis=0, keepdims=True) for mi in range(2)]
        o = acc_ref[0] / l[0] - lam * (acc_ref[1] / l[1])
        o = o * lax.rsqrt(jnp.mean(o * o, axis=0, keepdims=True) + EPS)
        o = (o * g_ref[...]) * (1.0 - LAM_INIT0)
        o_ref[pl.ds(pl.multiple_of(t * TQ, TQ), TQ), :] = o.T.astype(o_ref.dtype)

    def col_max(m8):
        return [jnp.max(m8[mi], axis=0, keepdims=True) for mi in range(2)]

    def stage(t_score, s_write, t_prob, s_read, m_read):
        qs = load_q(t_score)
        m8 = l8 = None
        for j in range(nblk + 1):
            m8 = scores_block(qs, j, s_write, m8)
            l8 = probs_block(j, s_read, m_read, l8)
        finish(t_prob, l8)
        return col_max(m8)

    qs0 = load_q(0)
    m8 = None
    for j in range(nblk + 1):
        m8 = scores_block(qs0, j, sa_ref, m8)
    m_a0 = col_max(m8)

    ntile = S // TQ
    assert ntile % 2 == 0

    def body(t, m):
        return lax.cond(t % 2 == 0,
                        lambda m: tuple(stage(t + 1, sb_ref, t, sa_ref, m)),
                        lambda m: tuple(stage(t + 1, sa_ref, t, sb_ref, m)), m)

    m_last = lax.fori_loop(0, ntile - 1, body, tuple(m_a0))
    l8 = None
    for j in range(nblk + 1):
        l8 = probs_block(j, sb_ref, m_last, l8)
    finish(ntile - 1, l8)


def _attn_call(lam_p, subln_g_col, q, k, vt, kc, vct):
    qspec = pl.BlockSpec((None, S, 128), lambda b, h: (b, 0, h))
    return pl.pallas_call(
        _attn_kernel,
        grid=(B, HEADS),
        in_specs=[pl.BlockSpec((4, 64), lambda b, h: (0, 0)), pl.BlockSpec((128, 1), lambda b, h: (0, 0)),
                  qspec, qspec,
                  pl.BlockSpec((None, 128, S), lambda b, h: (b, h, 0)),
                  pl.BlockSpec((None, CTX, 128), lambda b, h: (b, 0, h)),
                  pl.BlockSpec((None, 128, CTX), lambda b, h: (b, h, 0))],
        out_specs=qspec,
        out_shape=jax.ShapeDtypeStruct((B, S, DD), BF16),
        scratch_shapes=[pltpu.VMEM((2, CTX + S, TQ), F32), pltpu.VMEM((2, CTX + S, TQ), F32),
                        pltpu.VMEM((2, 128, TQ), F32)],
        compiler_params=_params("parallel", "parallel"),
        name="diffattn",
    )(lam_p, subln_g_col, q, k, vt, kc, vct)


def _cplx_block(m):
    return np.block([[m.real, -m.imag], [m.imag, m.real]])


@functools.lru_cache(maxsize=None)
def _dft_tables():
    k = np.arange(R)
    w_r = np.exp(-2j * np.pi * np.outer(k, k) / R)
    f_a = _cplx_block(w_r[:, :HALF])
    f_a_real = np.concatenate([w_r.real, w_r.imag], axis=0)
    g_a = _cplx_block(np.conj(w_r)[:HALF, :])
    tw = np.exp(-2j * np.pi * np.outer(k, k) / NFFT)
    f32 = lambda a: np.ascontiguousarray(a, dtype=np.float32)
    return (f32(f_a), f32(f_a_real), f32(g_a), f32(w_r.real), f32(w_r.imag), f32(tw.real), f32(tw.imag))


@functools.lru_cache(maxsize=None)
def _filter_positions():
    idx = np.arange(NFFT)
    pos = np.where(idx < S, idx, NFFT - idx) % S
    bands = (FILTER_EMB - 1) // 2
    t = np.linspace(0.0, 1.0, S)[:, None]
    ang = (2.0 * math.pi / S) * np.arange(S)[:, None] * np.linspace(1e-4, bands - 1, bands)[None, :]
    z = np.concatenate([t, np.cos(ang), -np.sin(ang)], axis=-1)
    z = np.pad(z, ((0, 0), (0, FILTER_EMB_PAD - FILTER_EMB)))
    return np.ascontiguousarray(z[pos], dtype=np.float32)


def _decay_rates():
    return jnp.abs(jnp.linspace(math.log(1e-2) / 1.5, math.log(1e-2) / 0.3, DH, dtype=F32))[None, :]


@functools.lru_cache(maxsize=None)
def _rope_tables():
    rows = S // GRID_W
    row = np.repeat(np.arange(rows), GRID_W).astype(np.float64)
    col = np.tile(np.arange(GRID_W), rows).astype(np.float64)
    m = 16
    inv = 10000.0 ** (-np.arange(m) / m)
    ang_r = row[:, None] * inv[None, :]
    ang_c = col[:, None] * inv[None, :]
    cos = np.concatenate([np.cos(ang_r)] * 2 + [np.cos(ang_c)] * 2, axis=1)
    sin = np.concatenate([-np.sin(ang_r), np.sin(ang_r), -np.sin(ang_c), np.sin(ang_c)], axis=1)
    f32 = lambda a: np.ascontiguousarray(np.concatenate([a, a], axis=1), dtype=np.float32)
    return f32(cos), f32(sin)


TR_F = 512


def _filter_kernel(zt_ref, t_ref, w1t_ref, b1_ref, fr_ref, w2t_ref, b2_ref, w3_ref, dec_ref, o_ref):
    i = pl.program_id(0)
    fr = fr_ref[...]
    h = jnp.sin(fr[:, 0:1] * (_dot3(w1t_ref[...], zt_ref[...]) + b1_ref[...]))
    h = jnp.sin(fr[:, 1:2] * (_dot3(w2t_ref[...], h) + b2_ref[...]))
    h = h.T
    window = jnp.exp(-t_ref[...] * dec_ref[...]) + 0.05
    ridx = i * TR_F + lax.broadcasted_iota(jnp.int32, (TR_F, 1), 0)
    for o in range(2):
        o_ref[o] = jnp.where(ridx == S, 0.0, _dot3(h, w3_ref[o]) * window)


def _filter_call(zt, tcol, w1t, b1c, freqc, w2t, b2c, w3r, dec):
    full = lambda shape: pl.BlockSpec(shape, lambda i: (0,) * len(shape))
    half_tiles = S // TR_F
    return pl.pallas_call(
        _filter_kernel,
        grid=(NFFT // TR_F,),
        in_specs=[pl.BlockSpec((FILTER_EMB_PAD, TR_F), lambda i: (0, i)), pl.BlockSpec((TR_F, 1), lambda i: (i, 0)),
                  full((FILTER_WIDTH, FILTER_EMB_PAD)), full((FILTER_WIDTH, 1)), full((FILTER_WIDTH, 2)),
                  full((FILTER_WIDTH, FILTER_WIDTH)), full((FILTER_WIDTH, 1)),
                  pl.BlockSpec((2, None, FILTER_WIDTH, DH), lambda i: (0, i // half_tiles, 0, 0)),
                  full((1, DH))],
        out_specs=pl.BlockSpec((2, TR_F, DH), lambda i: (0, i, 0)),
        out_shape=jax.ShapeDtypeStruct((2, NFFT, DH), F32),
        compiler_params=_params("parallel"),
        name="hyena_filter",
    )(zt, tcol, w1t, b1c, freqc, w2t, b2c, w3r, dec)


NB = 8


def _split_bf16(a):
    hi = a.astype(BF16)
    return hi, (a - hi.astype(F32)).astype(BF16)


def _dot3(a, b):
    a_hi, a_lo = _split_bf16(a)
    b_hi, b_lo = _split_bf16(b)
    return _dot(a_hi, b_hi) + (_dot(a_hi, b_lo) + _dot(a_lo, b_hi))


def _fspec_a_kernel(f_ref, h_ref, o_ref, x_ref):
    f_hi, f_lo = _split_bf16(f_ref[...])
    for j in range(NB):
        x_ref[j % 2] = h_ref[:, j, :]
        x_hi, x_lo = _split_bf16(x_ref[j % 2])
        o_ref[j] = _dot(f_hi, x_hi) + (_dot(f_hi, x_lo) + _dot(f_lo, x_hi))


def _fspec_a_call(f_a_real, hview):
    return pl.pallas_call(
        _fspec_a_kernel,
        grid=(2, R // NB),
        in_specs=[pl.BlockSpec((2 * R, R), lambda o, j: (0, 0)),
                  pl.BlockSpec((None, R, NB, DH), lambda o, j: (o, 0, j, 0))],
        out_specs=pl.BlockSpec((None, NB, 2 * R, DH), lambda o, j: (o, j, 0, 0)),
        out_shape=jax.ShapeDtypeStruct((2, R, 2 * R, DH), F32),
        scratch_shapes=[pltpu.VMEM((2, R, DH), F32)],
        compiler_params=_params("parallel", "parallel"),
        name="hyena_fspec_a",
    )(f_a_real, hview)


def _rows2(ref, j):
    return jnp.concatenate([ref[:, 0, j, :], ref[:, 1, j, :]], axis=0)


def _conv_a_kernel(f_ref, z_ref, o_ref):
    f = f_ref[...]
    for j in range(NB):
        zz = jnp.concatenate([z_ref[0, :, j, :], z_ref[1, :, j, :]], axis=0).astype(BF16)
        o_ref[j] = _dot(f, zz)


def _conv_a_call(f_a, zview):
    return pl.pallas_call(
        _conv_a_kernel,
        grid=(2, R // NB),
        in_specs=[pl.BlockSpec((2 * R, R), lambda p, j: (0, 0)),
                  pl.BlockSpec((2, None, HALF, NB, DH), lambda p, j: (0, p, 0, j, 0))],
        out_specs=pl.BlockSpec((None, NB, 2 * R, DH), lambda p, j: (p, j, 0, 0)),
        out_shape=jax.ShapeDtypeStruct((2, R, 2 * R, DH), F32),
        compiler_params=_params("parallel", "parallel"),
        name="hyena_conv_a",
    )(f_a, zview)


def _conv_c_kernel(wr_ref, wi_ref, tr_ref, ti_ref, af_ref, a_ref, o_ref, x_ref):
    wr, wi = wr_ref[...], wi_ref[...]
    for kk in range(NB):
        tr, ti = tr_ref[kk:kk + 1, :], ti_ref[kk:kk + 1, :]
        mr = wr * tr - wi * ti
        mi = wr * ti + wi * tr
        mb = jnp.concatenate([jnp.concatenate([mr, -mi], axis=1), jnp.concatenate([mi, mr], axis=1)], axis=0)
        m_hi, m_lo = _split_bf16(mb)
        x_ref[kk % 2] = _rows2(af_ref, kk)
        f_hi, f_lo = _split_bf16(x_ref[kk % 2])
        h = (_dot(m_hi, f_hi) + (_dot(m_hi, f_lo) + _dot(m_lo, f_hi))) * (1.0 / NFFT)
        hr, hi = h[0:R], h[R:2 * R]
        a = jnp.concatenate([_rows2(a_ref.at[0], kk), _rows2(a_ref.at[1], kk)], axis=1).astype(BF16)
        x = _dot(m_hi, a)
        ys = []
        for p in range(2):
            xr, xi = x[0:R, p * DH:(p + 1) * DH], x[R:2 * R, p * DH:(p + 1) * DH]
            ys.append(jnp.concatenate([xr * hr - xi * hi, xr * hi + xi * hr], axis=0))
        b = _dot(mb.T.astype(BF16), jnp.concatenate(ys, axis=1).astype(BF16))
        o_ref[0, kk] = b[:, 0:DH]
        o_ref[1, kk] = b[:, DH:2 * DH]


def _conv_c_call(w_re, w_im, tw_re, tw_im, af_view, order, a_view):
    full = pl.BlockSpec((R, R), lambda k: (0, 0))
    trow = pl.BlockSpec((NB, R), lambda k: (k, 0))
    return pl.pallas_call(
        _conv_c_kernel,
        grid=(R // NB,),
        in_specs=[full, full, trow, trow,
                  pl.BlockSpec((None, R, 2, NB, DH), lambda k: (order, 0, 0, k, 0)),
                  pl.BlockSpec((2, R, 2, NB, DH), lambda k: (0, 0, 0, k, 0))],
        out_specs=pl.BlockSpec((2, NB, 2 * R, DH), lambda k: (0, k, 0, 0)),
        out_shape=jax.ShapeDtypeStruct((2, R, 2 * R, DH), F32),
        scratch_shapes=[pltpu.VMEM((2, 2 * R, DH), F32)],
        compiler_params=_params("parallel"),
        name="hyena_conv_c",
    )(w_re, w_im, tw_re, tw_im, af_view, a_view)


def _conv_out_kernel(first_order, g_ref, f_ref, b_ref, z_ref, gate_ref, d_ref, zo_ref, *next_a):
    g, d = g_ref[...], d_ref[...]
    for j in range(NB):
        y = _dot(g, _rows2(b_ref, j).astype(BF16))
        zn = []
        for bi in range(2):
            zc = z_ref[bi, :, j, :] if first_order else z_ref[bi, j]
            zn.append(gate_ref[bi, :, j, :] * (y[bi * HALF:(bi + 1) * HALF] + d * zc))
            zo_ref[bi, j] = zn[bi]
        if first_order:
            next_a[0][j] = _dot(f_ref[...], jnp.concatenate(zn, axis=0).astype(BF16))


def _conv_out_call(g_a, f_a, b_view, z_in, gate_view, drow, first_order):
    pm = pl.BlockSpec((2, None, HALF, NB, DH), lambda p, j: (0, p, 0, j, 0))
    nm = pl.BlockSpec((2, None, NB, HALF, DH), lambda p, j: (0, p, j, 0, 0))
    out_specs = [nm]
    out_shape = [jax.ShapeDtypeStruct((2, 2, R, HALF, DH), F32)]
    if first_order:
        out_specs.append(pl.BlockSpec((None, NB, 2 * R, DH), lambda p, j: (p, j, 0, 0)))
        out_shape.append(jax.ShapeDtypeStruct((2, R, 2 * R, DH), F32))
    return pl.pallas_call(
        functools.partial(_conv_out_kernel, first_order),
        grid=(2, R // NB),
        in_specs=[pl.BlockSpec((R, 2 * R), lambda p, j: (0, 0)), pl.BlockSpec((2 * R, R), lambda p, j: (0, 0)),
                  pl.BlockSpec((None, R, 2, NB, DH), lambda p, j: (p, 0, 0, j, 0)),
                  pm if first_order else nm, pm, pl.BlockSpec((1, DH), lambda p, j: (0, 0))],
        out_specs=out_specs,
        out_shape=out_shape,
        compiler_params=_params("parallel", "parallel"),
        name="hyena_conv_out0" if first_order else "hyena_conv_out1",
    )(g_a, f_a, b_view, z_in, gate_view, drow)


TM_OUT = NB * R


def _outproj_kernel(x_ref, ya_ref, yb_ref, wa_ref, wb_ref, g1_ref, o_ref):
    ya = jnp.concatenate([ya_ref[:, r, :] for r in range(NB)], axis=0).astype(BF16)
    mix = _dot(ya, wa_ref[...]) + _dot(yb_ref[...], wb_ref[...])
    o_ref[...] = x_ref[...] + g1_ref[...] * mix


def _outproj_call(x2, ya_nm, yb, wa, wb, g1):
    tm = TM_OUT
    tiles = S // tm
    row = lambda i: (i, 0)
    return pl.pallas_call(
        _outproj_kernel,
        grid=(T // tm,),
        in_specs=[pl.BlockSpec((tm, D), row),
                  pl.BlockSpec((None, R, NB, DH), lambda i: (i // tiles, 0, i % tiles, 0)),
                  pl.BlockSpec((tm, DD), row),
                  pl.BlockSpec((DH, D), lambda i: (0, 0)), pl.BlockSpec((DD, D), lambda i: (0, 0)),
                  pl.BlockSpec((None, 1, D), lambda i: (i // tiles, 0, 0))],
        out_specs=pl.BlockSpec((tm, D), row),
        out_shape=jax.ShapeDtypeStruct((T, D), F32),
        compiler_params=_params("parallel"),
        name="outproj",
    )(x2, ya_nm, yb, wa, wb, g1)


TM_FFN = 1024
FH = 16
CK = 256
assert D_FF % CK == 0


def _ffn_kernel(final, xp_ref, xm_ref, xn_ref, g_ref, sh_ref, sc_ref, g2_ref, wup_ref, cw_ref, cb_ref, wdn_ref,
                fg_ref, o_ref, hn_ref, act_ref):
    i = pl.program_id(0)
    tiles = S // TM_FFN
    first = (i % tiles) == 0
    last = (i % tiles) == tiles - 1
    g, sh, sc = g_ref[...], sh_ref[...], sc_ref[...]
    xm = xm_ref[...]
    hn_ref[0:FH, :] = jnp.where(first, 0.0, _rms_mod(xp_ref[...], g, sh, sc)).astype(BF16)
    hn_ref[FH:FH + TM_FFN, :] = _rms_mod(xm, g, sh, sc).astype(BF16)
    hn_ref[FH + TM_FFN:2 * FH + TM_FFN, :] = jnp.where(last, 0.0, _rms_mod(xn_ref[...], g, sh, sc)).astype(BF16)
    hn = hn_ref[...]

    rows = TM_FFN + 2 * FH

    def conv(u, c0):
        w = cw_ref[:, c0:c0 + CK]
        prev = pltpu.roll(u, 1, 0)[FH:FH + TM_FFN]
        nxt = pltpu.roll(u, rows - 1, 0)[FH:FH + TM_FFN]
        return w[0:1] * prev + w[1:2] * u[FH:FH + TM_FFN] + w[2:3] * nxt + cb_ref[:, c0:c0 + CK]

    for c in range(D_FF // CK):
        c0 = c * CK
        ug = _dot(hn, wup_ref[:, c0:c0 + CK])
        uu = _dot(hn, wup_ref[:, D_FF + c0:D_FF + c0 + CK])
        act_ref[:, c0:c0 + CK] = (_silu(conv(ug, c0)) * conv(uu, D_FF + c0)).astype(BF16)
    out = xm + g2_ref[...] * _dot(act_ref[...], wdn_ref[...])
    if final:
        out = (out * lax.rsqrt(jnp.mean(out * out, axis=-1, keepdims=True) + EPS)) * fg_ref[...]
    o_ref[...] = out


def _ffn_call(x2, g, sh, sc, g2, wup, cw, cb, wdn, fg, final):
    tm = TM_FFN
    tiles = S // tm
    nh = tm // FH
    row = lambda i: (i, 0)
    bvec = pl.BlockSpec((None, 1, D), lambda i: (i // tiles, 0, 0))
    full = lambda shape: pl.BlockSpec(shape, lambda i: (0,) * len(shape))
    once = lambda shape: pl.BlockSpec(shape, lambda i: (0,) * len(shape), pipeline_mode=pl.Buffered(1))
    return pl.pallas_call(
        functools.partial(_ffn_kernel, final),
        grid=(T // tm,),
        in_specs=[
            pl.BlockSpec((FH, D), lambda i: (jnp.maximum(i * nh - 1, 0), 0)),
            pl.BlockSpec((tm, D), row),
            pl.BlockSpec((FH, D), lambda i: (jnp.minimum((i + 1) * nh, T // FH - 1), 0)),
            full((1, D)), bvec, bvec, bvec,
            once((D, 2 * D_FF)), full((3, 2 * D_FF)), full((1, 2 * D_FF)), once((D_FF, D)), full((1, D)),
        ],
        out_specs=pl.BlockSpec((tm, D), row),
        out_shape=jax.ShapeDtypeStruct((T, D), F32),
        scratch_shapes=[pltpu.VMEM((tm + 2 * FH, D), BF16), pltpu.VMEM((tm, D_FF), BF16)],
        compiler_params=_params("parallel"),
        name="convffn_final" if final else "convffn",
    )(x2, x2, x2, g, sh, sc, g2, wup, cw, cb, wdn, fg)


TM_SGU = 1024
SUB_SGU = 512


def _sgu_kernel(x_ref, g_ref, sh_ref, sc_ref, g1_ref, win_ref, bin_ref, lng_ref, lnb_ref, ws_ref, bs_ref, wout_ref,
                o_ref, s_ref):
    cg = D // GROUPS
    for r0 in range(0, TM_SGU, SUB_SGU):
        x = x_ref[r0:r0 + SUB_SGU, :]
        hn = _rms_mod(x, g_ref[...], sh_ref[...], sc_ref[...]).astype(BF16)
        pre = _dot(hn, win_ref[...]) + bin_ref[...]
        act = 0.5 * pre * (1.0 + lax.erf(pre * (2.0 ** -0.5)))
        u = act[:, 0:D]
        v = act[:, D:2 * D]
        mu = jnp.mean(v, axis=-1, keepdims=True)
        vc = v - mu
        v = (vc * lax.rsqrt(jnp.mean(vc * vc, axis=-1, keepdims=True) + EPS)) * lng_ref[...] + lnb_ref[...]
        vb = v.astype(BF16)
        for n in range(SUB_SGU // CHUNK):
            for gi in range(GROUPS):
                s_ref[r0 + n * CHUNK:r0 + (n + 1) * CHUNK, gi * cg:(gi + 1) * cg] = (
                    _dot(ws_ref[gi], vb[n * CHUNK:(n + 1) * CHUNK, gi * cg:(gi + 1) * cg])
                    + bs_ref[:, gi * cg:(gi + 1) * cg])
        gated = (u * s_ref[r0:r0 + SUB_SGU, :]).astype(BF16)
        o_ref[r0:r0 + SUB_SGU, :] = x + g1_ref[...] * _dot(gated, wout_ref[...])


def _sgu_call(x2, g, sh, sc, g1, win, bin_, lng, lnb, ws, bs_full, wout):
    tm = TM_SGU
    tiles = S // tm
    row = lambda i: (i, 0)
    bvec = pl.BlockSpec((None, 1, D), lambda i: (i // tiles, 0, 0))
    full = lambda shape: pl.BlockSpec(shape, lambda i: (0,) * len(shape))
    return pl.pallas_call(
        _sgu_kernel,
        grid=(T // tm,),
        in_specs=[pl.BlockSpec((tm, D), row), full((1, D)), bvec, bvec, bvec,
                  full((D, 2 * D)), full((1, 2 * D)), full((1, D)), full((1, D)),
                  full((GROUPS, CHUNK, CHUNK)), full((CHUNK, D)), full((D, D))],
        out_specs=pl.BlockSpec((tm, D), row),
        out_shape=jax.ShapeDtypeStruct((T, D), F32),
        scratch_shapes=[pltpu.VMEM((tm, D), F32)],
        compiler_params=_params("parallel"),
        name="sgu",
    )(x2, g, sh, sc, g1, win, bin_, lng, lnb, ws, bs_full, wout)


def kernel(x, c, ctx, c_ctx, mod_w, mod_b, norm_mix_g, norm_ffn_g, ffn_w_up, ffn_conv_w, ffn_conv_b, ffn_w_down,
           ab_w_in, hy_conv_w, hy_conv_b, hy_w1, hy_b1, hy_freq, hy_w2, hy_b2, hy_w3, hy_bias, da_lambda,
           da_subln_g, ab_w_out, sgu_w_in, sgu_b_in, sgu_ln_g, sgu_ln_b, sgu_w_s, sgu_b_s, sgu_w_out,
           final_norm_g):
    x2 = x.reshape(T, D)
    cc = jnp.concatenate([c, c_ctx[None, :], jnp.zeros((3, D), F32)], axis=0)
    mods = _mod_call(cc, mod_w, mod_b)

    def mod_vec(layer, k):
        return mods[layer, 0:B, k * D:(k + 1) * D].reshape(B, 1, D)

    w_in = ab_w_in[0].astype(BF16)
    n_hq = 3 * DH + DD
    cos_t, sin_t = (jnp.asarray(t) for t in _rope_tables())
    w_k = w_in[:, n_hq:n_hq + DD]
    w_vt = w_in[:, n_hq + DD:].T
    hv, hx1, hx2, q, k, vt = _inproj_call(
        x2, norm_mix_g[0][None, :], mod_vec(0, 0), mod_vec(0, 1),
        w_in[:, 0:3 * DH], jnp.concatenate([w_in[:, 3 * DH:n_hq], w_k], axis=1), w_vt,
        cos_t, sin_t, hy_conv_w[0], hy_conv_b[0][None, :])
    kc, vct = _ctxkv_call(ctx.reshape(B * CTX, D), norm_mix_g[0][None, :], mods[0, 4:5, 0:D], mods[0, 4:5, D:2 * D],
                          w_k, w_vt)
    yb = _attn_call(da_lambda[0], da_subln_g[0][:, None], q.reshape(B, S, DD), k.reshape(B, S, DD), vt,
                    kc.reshape(B, CTX, DD), vct)

    f_a, f_a_real, g_a, w_re, w_im, tw_re, tw_im = (jnp.asarray(t) for t in _dft_tables())
    w1p = jnp.pad(hy_w1[0], ((0, FILTER_EMB_PAD - FILTER_EMB), (0, 0)))
    w3r = hy_w3[0].reshape(FILTER_WIDTH, 2, 2, DH).transpose(1, 2, 0, 3)
    zpos = _filter_positions()
    filt = _filter_call(jnp.asarray(np.ascontiguousarray(zpos.T)), jnp.asarray(zpos[:, 0:1]), w1p.T,
                        hy_b1[0][:, None], hy_freq[0].T, hy_w2[0].T, hy_b2[0][:, None], w3r, _decay_rates())
    af_view = _fspec_a_call(f_a_real, filt.reshape(2, R, R, DH)).reshape(2, R, 2, R, DH)

    f_a_b, g_a_b = f_a.astype(BF16), g_a.astype(BF16)
    pm = lambda t: t.reshape(2, 2, HALF, R, DH)
    a = _conv_a_call(f_a_b, pm(hv))
    bq = _conv_c_call(w_re, w_im, tw_re, tw_im, af_view, 0, a.reshape(2, R, 2, R, DH))
    z1, a = _conv_out_call(g_a_b, f_a_b, bq.reshape(2, R, 2, R, DH), pm(hv), pm(hx1), hy_bias[0, 0][None, :], True)
    bq = _conv_c_call(w_re, w_im, tw_re, tw_im, af_view, 1, a.reshape(2, R, 2, R, DH))
    (z2,) = _conv_out_call(g_a_b, f_a_b, bq.reshape(2, R, 2, R, DH), z1, pm(hx2), hy_bias[0, 1][None, :], False)

    w_out = ab_w_out[0].astype(BF16)
    x2 = _outproj_call(x2, z2.reshape(B, R, HALF, DH), yb.reshape(T, DD), w_out[0:DH], w_out[DH:], mod_vec(0, 2))
    x2 = _ffn_call(x2, norm_ffn_g[0][None, :], mod_vec(0, 3), mod_vec(0, 4), mod_vec(0, 5),
                   ffn_w_up[0].astype(BF16), ffn_conv_w[0], ffn_conv_b[0][None, :], ffn_w_down[0].astype(BF16),
                   final_norm_g[None, :], False)

    bs_full = jnp.repeat(sgu_b_s[0].T, D // GROUPS, axis=1)
    x2 = _sgu_call(x2, norm_mix_g[1][None, :], mod_vec(1, 0), mod_vec(1, 1), mod_vec(1, 2),
                   sgu_w_in[0].astype(BF16), sgu_b_in[0][None, :], sgu_ln_g[0][None, :], sgu_ln_b[0][None, :],
                   sgu_w_s[0].astype(BF16), bs_full, sgu_w_out[0].astype(BF16))
    x2 = _ffn_call(x2, norm_ffn_g[1][None, :], mod_vec(1, 3), mod_vec(1, 4), mod_vec(1, 5),
                   ffn_w_up[1].astype(BF16), ffn_conv_w[1], ffn_conv_b[1][None, :], ffn_w_down[1].astype(BF16),
                   final_norm_g[None, :], True)
    return x2.reshape(B, S, D)
```
